```python
import math
import jax
import jax.numpy as jnp
from jax import lax
import numpy as np

D_MODEL = 2048
BATCH = 4
SEQ = 8192
DEPTH = 4

CTX_LEN = 256
GRID_W = 64
HEAD_DIM = 128
NORM_EPS = 1e-6
NEG_INF = -1e30
ROPE_THETA = 10000.0
F32 = jnp.float32

NA_HEADS = D_MODEL // (2 * HEAD_DIM)
NA_WIDTH = NA_HEADS * HEAD_DIM
NA_WIN_ROWS = 8
NA_WIN_COLS = 16

GLA_HEADS = 4
GLA_DK = 128
GLA_DV = D_MODEL // (2 * GLA_HEADS)
GLA_QK = GLA_HEADS * GLA_DK
GLA_V = GLA_HEADS * GLA_DV
GLA_LOWRANK = 16
GLA_GATE_NORMALIZER = 16.0
GLA_CHUNK = 64
GLA_DECAY_UNIQUE = GLA_HEADS * GLA_DK // 2

EVEN_IN = 3 * NA_WIDTH + 2 * GLA_QK + 2 * GLA_V + 2 * GLA_LOWRANK
EVEN_MIX = NA_WIDTH + GLA_V

HY_WIDTH = D_MODEL // 2
HY_ORDER = 2
HY_EMB = 33
HY_FFN = 64
HY_SIN_FREQ = 1.0
HY_DECAY_TARGET = 1e-2
HY_FAST_DECAY = 0.3
HY_SLOW_DECAY = 1.5
SC_WIDTH = D_MODEL // 2
CONV_W = 3
ODD_IN = (HY_ORDER + 1) * HY_WIDTH + 3 * SC_WIDTH
ODD_MIX = HY_WIDTH + SC_WIDTH

N_GROUPS = 4
EXPERTS_PER_GROUP = 8
N_EXPERTS = N_GROUPS * EXPERTS_PER_GROUP
TOP_K_IN_GROUP = 2
D_EXPERT = 512
MOE_BLOCK = 128

kernel_name = 'hybrid_na_gla_hyena_shortconv_hmoe_dit'


def rms_norm(x, gain):
    xf = x.astype(F32)
    y = xf * lax.rsqrt(jnp.mean(xf * xf, axis=-1, keepdims=True) + NORM_EPS)
    return (y * gain.astype(F32)).astype(x.dtype)


def modulate(h, shift, scale):
    return h * (1.0 + scale) + shift


def dwconv_centred(u, w):
    return lax.conv_general_dilated(
        u, w.astype(u.dtype)[:, None, :], window_strides=(1,),
        padding=[(CONV_W // 2, CONV_W // 2)],
        dimension_numbers=('NWC', 'WIO', 'NWC'),
        feature_group_count=u.shape[-1])


def axial_rope(t):
    L, dh = t.shape[1], t.shape[-1]
    half = dh // 2
    nf = half // 2
    pos = jnp.arange(L)
    inv_freq = ROPE_THETA ** (-jnp.arange(nf, dtype=F32) / nf)

    def rotate(xp, p):
        ang = p.astype(F32)[:, None] * inv_freq[None, :]
        cos = jnp.cos(ang)[None, :, None, :]
        sin = jnp.sin(ang)[None, :, None, :]
        x1, x2 = xp[..., :nf], xp[..., nf:]
        return jnp.concatenate([x1 * cos - x2 * sin, x2 * cos + x1 * sin], axis=-1)

    tf = t.astype(F32)
    return jnp.concatenate([rotate(tf[..., :half], pos // GRID_W),
                            rotate(tf[..., half:], pos % GRID_W)], axis=-1)


def neighbourhood_attention(q, k, v, k_ctx, v_ctx, rpb):
    B, S, H, dh = q.shape
    rows = S // GRID_W
    kr = min(NA_WIN_ROWS, rows)
    kc = NA_WIN_COLS
    r = jnp.arange(rows)
    row_start = jnp.clip(r - kr // 2, 0, rows - kr)
    row_idx = row_start[:, None] + jnp.arange(kr)[None, :]
    cols = jnp.arange(GRID_W)
    col_start = jnp.clip(cols - kc // 2, 0, GRID_W - kc)
    col_ok = (cols[None, :] >= col_start[:, None]) & (cols[None, :] < col_start[:, None] + kc)
    roff = row_idx - r[:, None] + NA_WIN_ROWS - 1
    coff = jnp.clip(cols[None, :] - cols[:, None], -(kc - 1), kc - 1) + kc - 1
    bias = jnp.take(rpb.astype(F32)[:, roff], coff, axis=-1)
    bias = jnp.where(col_ok[None, None, None], bias, NEG_INF).transpose(0, 1, 3, 2, 4)
    scale = dh ** -0.5
    qg = q.reshape(B, rows, GRID_W, H, dh)
    kg = k.reshape(B, rows, GRID_W, H, dh)[:, row_idx]
    vg = v.reshape(B, rows, GRID_W, H, dh)[:, row_idx]
    s_loc = jnp.einsum('brqhd,brkwhd->bhrqkw', qg, kg, preferred_element_type=F32) * scale + bias[None]
    s_ctx = jnp.einsum('brqhd,bchd->bhrqc', qg, k_ctx, preferred_element_type=F32) * scale
    m = jnp.maximum(s_loc.max(axis=(-2, -1)), s_ctx.max(axis=-1))
    p_loc = jnp.exp(s_loc - m[..., None, None])
    p_ctx = jnp.exp(s_ctx - m[..., None])
    den = p_loc.sum(axis=(-2, -1)) + p_ctx.sum(axis=-1)
    o = (jnp.einsum('bhrqkw,brkwhd->brqhd', p_loc, vg.astype(F32))
         + jnp.einsum('bhrqc,bchd->brqhd', p_ctx, v_ctx.astype(F32)))
    o = o / den.transpose(0, 2, 3, 1)[..., None]
    return o.reshape(B, S, H * dh).astype(q.dtype)


def context_attention(q, k, v):
    B, L, H, dh = q.shape
    s = jnp.einsum('bqhd,bkhd->bhqk', q, k, preferred_element_type=F32) * dh ** -0.5
    p = jax.nn.softmax(s, axis=-1)
    o = jnp.einsum('bhqk,bkhd->bqhd', p, v.astype(F32))
    return o.reshape(B, L, H * dh).astype(q.dtype)


def tie_rotary_pairs(u):
    lead = u.shape[:-1]
    u = u.reshape(lead + (GLA_HEADS, 2, 1, GLA_DK // 4))
    u = jnp.broadcast_to(u, lead + (GLA_HEADS, 2, 2, GLA_DK // 4))
    return u.reshape(lead + (GLA_HEADS, GLA_DK))


def gla_chunked(q, k, v, log_a, s0):
    B, H, L, dk = q.shape
    dv = v.shape[-1]
    C = GLA_CHUNK
    n = L // C
    q, k, log_a = (t.reshape(B, H, n, C, dk) for t in (q, k, log_a))
    v = v.reshape(B, H, n, C, dv)
    b = jnp.cumsum(log_a, axis=3)
    b_last = b[:, :, :, -1:, :]
    b_mid = b[:, :, :, C // 2 - 1:C // 2, :]
    att = jnp.einsum('bhncd,bhnsd->bhncs', q * jnp.exp(b - b_mid), k * jnp.exp(b_mid - b))
    att = jnp.where(jnp.tril(jnp.ones((C, C), dtype=bool)), att, 0.0)
    o = jnp.einsum('bhncs,bhnsv->bhncv', att, v)
    d_state = jnp.einsum('bhncd,bhncv->bhndv', k * jnp.exp(b_last - b), v)
    decay = jnp.exp(b_last[:, :, :, 0, :])

    def step(s, inp):
        ds, dec = inp
        return dec[..., None] * s + ds, s

    s_fin, s_start = lax.scan(step, s0, (jnp.moveaxis(d_state, 2, 0), jnp.moveaxis(decay, 2, 0)))
    o = o + jnp.einsum('bhncd,bhndv->bhncv', q * jnp.exp(b), jnp.moveaxis(s_start, 0, 2))
    return o.reshape(B, H, L, dv), s_fin


def gla_mixer(p_lat, p_ctx, ctx_out, gk_up, gk_bias, gnorm):
    def prep(p, latent):
        q, k, v, g, lr = p
        B, L = q.shape[:2]
        q = q.reshape(B, L, GLA_HEADS, GLA_DK)
        k = k.reshape(B, L, GLA_HEADS, GLA_DK)
        if latent:
            q, k = axial_rope(q), axial_rope(k)
        z = jnp.einsum('blnr,nru->blnu', lr.reshape(B, L, 2, GLA_LOWRANK).astype(F32),
                       gk_up.astype(F32)) + gk_bias.astype(F32)
        log_a = tie_rotary_pairs(jax.nn.log_sigmoid(z) / GLA_GATE_NORMALIZER)
        bh = lambda a: jnp.swapaxes(a.astype(F32), 1, 2)
        return (bh(q) * GLA_DK ** -0.5, bh(k), bh(v.reshape(B, L, GLA_HEADS, GLA_DV)),
                bh(log_a[:, :, 0]), bh(log_a[:, :, 1]), g)

    qc, kc, vc, fc, bc, gc = prep(p_ctx, False)
    ql, kl, vl, fl, bl, gl = prep(p_lat, True)
    zero = jnp.zeros(qc.shape[:2] + (GLA_DK, GLA_DV), F32)
    flip = lambda a: jnp.flip(a, axis=2)
    oc_f, s_f = gla_chunked(qc, kc, vc, fc, zero)
    ol_f, _ = gla_chunked(ql, kl, vl, fl, s_f)
    oc_b, s_b = gla_chunked(flip(qc), flip(kc), flip(vc), flip(bc), zero)
    ol_b, _ = gla_chunked(flip(ql), flip(kl), flip(vl), flip(bl), s_b)

    def finish(o, g):
        o = jnp.swapaxes(o, 1, 2)
        o = o * lax.rsqrt(jnp.mean(o * o, axis=-1, keepdims=True) + NORM_EPS) * gnorm.astype(F32)
        return (o.reshape(o.shape[0], o.shape[1], GLA_V) * jax.nn.silu(g.astype(F32))).astype(g.dtype)

    y_lat = finish(ol_f + flip(ol_b), gl)
    y_ctx = finish(oc_f + flip(oc_b), gc) if ctx_out else None
    return y_lat, y_ctx


def even_mixer(h, hc, ctx_out, w_in, w_out, rpb, gk_up, gk_bias, gnorm):
    bounds = [NA_WIDTH, 2 * NA_WIDTH, 3 * NA_WIDTH, 3 * NA_WIDTH + GLA_QK, 3 * NA_WIDTH + 2 * GLA_QK,
              3 * NA_WIDTH + 2 * GLA_QK + GLA_V, 3 * NA_WIDTH + 2 * GLA_QK + 2 * GLA_V]
    heads = lambda t: t.reshape(t.shape[0], t.shape[1], NA_HEADS, HEAD_DIM)
    pl = jnp.split(h @ w_in, bounds, axis=-1)
    pc = jnp.split(hc @ w_in, bounds, axis=-1)
    qa_l, ka_l, va_l = (heads(t) for t in pl[:3])
    qa_c, ka_c, va_c = (heads(t) for t in pc[:3])
    ya = neighbourhood_attention(qa_l, ka_l, va_l, ka_c, va_c, rpb)
    yb, yb_c = gla_mixer(pl[3:], pc[3:], ctx_out, gk_up, gk_bias, gnorm)
    y = jnp.concatenate([ya, yb], axis=-1) @ w_out
    if not ctx_out:
        return y, None
    ya_c = context_attention(qa_c, ka_c, va_c)
    return y, jnp.concatenate([ya_c, yb_c], axis=-1) @ w_out


def hyena_filters(L, w1, b1, w2, b2, w3, b3, w4):
    t = jnp.linspace(0.0, 1.0, L, dtype=F32)[:, None]
    bands = (HY_EMB - 1) // 2
    freqs = jnp.linspace(1e-4, bands - 1, bands, dtype=F32)[None, :]
    w = (2.0 * math.pi / L) * jnp.arange(L, dtype=F32)[:, None]
    emb = jnp.concatenate([t, jnp.cos(freqs * w), -jnp.sin(freqs * w)], axis=-1)
    a = jnp.sin(HY_SIN_FREQ * (emb @ w1.astype(F32) + b1.astype(F32)))
    a = jnp.sin(HY_SIN_FREQ * (a @ w2.astype(F32) + b2.astype(F32)))
    a = jnp.sin(HY_SIN_FREQ * (a @ w3.astype(F32) + b3.astype(F32)))
    h = (a @ w4.astype(F32)).reshape(L, HY_ORDER, 2, HY_WIDTH)
    deltas = jnp.abs(jnp.linspace(math.log(HY_DECAY_TARGET) / HY_SLOW_DECAY,
                                  math.log(HY_DECAY_TARGET) / HY_FAST_DECAY, HY_WIDTH, dtype=F32))
    h = h * jnp.exp(-t * deltas)[:, None, None, :]
    return h / (jnp.sum(jnp.abs(h), axis=(0, 2), keepdims=True) + NORM_EPS)


def bidir_long_conv(z, h_fwd, h_bwd, bias):
    L = z.shape[1]
    n = 2 * L
    Z = jnp.fft.rfft(z, n=n, axis=1)
    Hs = jnp.fft.rfft(h_fwd, n=n, axis=0) + jnp.conj(jnp.fft.rfft(h_bwd, n=n, axis=0))
    y = jnp.fft.irfft(Z * Hs[None], n=n, axis=1)[:, :L]
    return y + z * bias


def odd_mixer(h, hc, w_in, w_out, hy_short, hy_w1, hy_b1, hy_w2, hy_b2, hy_w3, hy_b3, hy_w4, hy_bias, sc_conv):
    n_hy = (HY_ORDER + 1) * HY_WIDTH

    def mix(u):
        L = u.shape[1]
        p = u @ w_in
        v, *gates = jnp.split(dwconv_centred(p[..., :n_hy], hy_short), HY_ORDER + 1, axis=-1)
        filt = hyena_filters(L, hy_w1, hy_b1, hy_w2, hy_b2, hy_w3, hy_b3, hy_w4)
        z = v.astype(F32)
        for n, gate in enumerate(gates):
            z = gate.astype(F32) * bidir_long_conv(z, filt[:, n, 0], filt[:, n, 1], hy_bias[n].astype(F32))
        b_gate, c_gate, x_in = jnp.split(p[..., n_hy:], 3, axis=-1)
        y_sc = b_gate * dwconv_centred(c_gate * x_in, sc_conv)
        return jnp.concatenate([z.astype(u.dtype), y_sc], axis=-1) @ w_out

    return mix(h), (None if hc is None else mix(hc))


def hier_moe(h, w_group, b_group, w_expert, b_expert, w1, w3, w2):
    N, D = h.shape
    hf = h.astype(F32)
    lg = hf @ w_group.astype(F32) + b_group.astype(F32)
    grp = jnp.argmax(lg, axis=-1)
    p_grp = jnp.take_along_axis(jax.nn.softmax(lg, axis=-1), grp[:, None], axis=-1)
    le = (hf @ w_expert.astype(F32) + b_expert.astype(F32)).reshape(N, N_GROUPS, EXPERTS_PER_GROUP)
    le = jnp.take_along_axis(le, grp[:, None, None], axis=1)[:, 0]
    top_v, top_i = lax.top_k(le, TOP_K_IN_GROUP)
    gate = p_grp * jax.nn.softmax(top_v, axis=-1)
    eid = grp[:, None] * EXPERTS_PER_GROUP + top_i
    A = N * TOP_K_IN_GROUP
    e_flat = eid.reshape(A)
    tok_flat = jnp.arange(A) // TOP_K_IN_GROUP
    order = jnp.argsort(e_flat)
    e_sorted = e_flat[order]
    counts = jax.ops.segment_sum(jnp.ones_like(e_flat), e_flat, num_segments=N_EXPERTS)
    padded = (counts + MOE_BLOCK - 1) // MOE_BLOCK * MOE_BLOCK
    start = jnp.cumsum(counts) - counts
    pstart = jnp.cumsum(padded) - padded
    slot = pstart[e_sorted] + jnp.arange(A) - start[e_sorted]
    n_blocks = -(-A // MOE_BLOCK) + N_EXPERTS
    cap = n_blocks * MOE_BLOCK
    slot_tok = jnp.full((cap,), N, dtype=jnp.int32).at[slot].set(tok_flat[order])
    h_pad = jnp.concatenate([h, jnp.zeros((1, D), h.dtype)], axis=0)
    xs = h_pad[slot_tok].reshape(n_blocks, MOE_BLOCK, D)
    block_expert = jnp.minimum(
        jnp.searchsorted(jnp.cumsum(padded), jnp.arange(n_blocks) * MOE_BLOCK, side='right'), N_EXPERTS - 1)

    def expert_block(args):
        xb, e = args
        return (jax.nn.silu(xb @ w1[e]) * (xb @ w3[e])) @ w2[e]

    ys = lax.map(expert_block, (xs, block_expert)).reshape(cap, D)
    contrib = ys[slot] * gate.reshape(A)[order][:, None].astype(ys.dtype)
    return jnp.zeros_like(h).at[tok_flat[order]].add(contrib)


def setup_inputs(seed: int = 0) -> dict:
    key = jax.random.key(seed)
    keys = iter(jax.random.split(key, 40))

    def rnd(shape, std):
        return jax.random.normal(next(keys), shape, F32) * std

    D = D_MODEL
    NL, NE, NO = DEPTH, (DEPTH + 1) // 2, DEPTH // 2
    return {
        'x': rnd((BATCH, SEQ, D), 1.0),
        'c': rnd((BATCH, D), 1.0),
        'ctx': rnd((BATCH, CTX_LEN, D), 1.0),
        'c_ctx': rnd((D,), 1.0),
        'w_mod': rnd((NL, D, 6 * D), 0.5 * D ** -0.5),
        'b_mod': rnd((NL, 6 * D), 0.02),
        'norm_mix': 1.0 + rnd((NL, D), 0.02),
        'norm_ffn': 1.0 + rnd((NL, D), 0.02),
        'norm_final': 1.0 + rnd((D,), 0.02),
        'w_in_even': rnd((NE, D, EVEN_IN), D ** -0.5),
        'w_out_even': rnd((NE, EVEN_MIX, D), EVEN_MIX ** -0.5),
        'na_rpb': rnd((NE, NA_HEADS, 2 * NA_WIN_ROWS - 1, 2 * NA_WIN_COLS - 1), 0.02),
        'gla_gk_up': rnd((NE, 2, GLA_LOWRANK, GLA_DECAY_UNIQUE), GLA_LOWRANK ** -0.5),
        'gla_gk_bias': rnd((NE, 2, GLA_DECAY_UNIQUE), 0.1),
        'gla_norm': 1.0 + rnd((NE, GLA_DV), 0.02),
        'w_in_odd': rnd((NO, D, ODD_IN), D ** -0.5),
        'w_out_odd': rnd((NO, ODD_MIX, D), ODD_MIX ** -0.5),
        'hy_short': rnd((NO, CONV_W, (HY_ORDER + 1) * HY_WIDTH), CONV_W ** -0.5),
        'hy_w1': rnd((NO, HY_EMB, HY_FFN), HY_EMB ** -0.5),
        'hy_b1': rnd((NO, HY_FFN), 0.02),
        'hy_w2': rnd((NO, HY_FFN, HY_FFN), HY_FFN ** -0.5),
        'hy_b2': rnd((NO, HY_FFN), 0.02),
        'hy_w3': rnd((NO, HY_FFN, HY_FFN), HY_FFN ** -0.5),
        'hy_b3': rnd((NO, HY_FFN), 0.02),
        'hy_w4': rnd((NO, HY_FFN, HY_ORDER * 2 * HY_WIDTH), HY_FFN ** -0.5),
        'hy_bias': rnd((NO, HY_ORDER, HY_WIDTH), 1.0),
        'sc_conv': rnd((NO, CONV_W, SC_WIDTH), CONV_W ** -0.5),
        'moe_w_group': rnd((NL, D, N_GROUPS), D ** -0.5),
        'moe_b_group': rnd((NL, N_GROUPS), 0.01),
        'moe_w_expert': rnd((NL, D, N_EXPERTS), D ** -0.5),
        'moe_b_expert': rnd((NL, N_EXPERTS), 0.01),
        'moe_w1': rnd((NL, N_EXPERTS, D, D_EXPERT), D ** -0.5),
        'moe_w3': rnd((NL, N_EXPERTS, D, D_EXPERT), D ** -0.5),
        'moe_w2': rnd((NL, N_EXPERTS, D_EXPERT, D), D_EXPERT ** -0.5),
    }


def reference(x, c, ctx, c_ctx, w_mod, b_mod, norm_mix, norm_ffn, norm_final,
              w_in_even, w_out_even, na_rpb, gla_gk_up, gla_gk_bias, gla_norm,
              w_in_odd, w_out_odd, hy_short, hy_w1, hy_b1, hy_w2, hy_b2, hy_w3, hy_b3, hy_w4, hy_bias, sc_conv,
              moe_w_group, moe_b_group, moe_w_expert, moe_b_expert, moe_w1, moe_w3, moe_w2):
    B, S, D = x.shape
    Lc = ctx.shape[1]
    last_even = 2 * ((DEPTH - 1) // 2)
    for l in range(DEPTH):
        even = (l % 2 == 0)
        ctx_out = l < last_even
        need_ctx = even or ctx_out
        mod = jnp.split((jax.nn.silu(c) @ w_mod[l] + b_mod[l])[:, None, :], 6, axis=-1)
        hx = modulate(rms_norm(x, norm_mix[l]), mod[0], mod[1])
        hc = None
        if need_ctx:
            cmod = jnp.split(jax.nn.silu(c_ctx) @ w_mod[l] + b_mod[l], 6)
            hc = modulate(rms_norm(ctx, norm_mix[l]), cmod[0], cmod[1])
        if even:
            e = l // 2
            y, yc = even_mixer(hx, hc, ctx_out, w_in_even[e], w_out_even[e], na_rpb[e],
                               gla_gk_up[e], gla_gk_bias[e], gla_norm[e])
        else:
            o = l // 2
            y, yc = odd_mixer(hx, hc if ctx_out else None, w_in_odd[o], w_out_odd[o], hy_short[o],
                              hy_w1[o], hy_b1[o], hy_w2[o], hy_b2[o], hy_w3[o], hy_b3[o], hy_w4[o],
                              hy_bias[o], sc_conv[o])
        x = x + mod[2] * y
        if ctx_out:
            ctx = ctx + cmod[2] * yc
        tokens = modulate(rms_norm(x, norm_ffn[l]), mod[3], mod[4]).reshape(B * S, D)
        if ctx_out:
            hc2 = modulate(rms_norm(ctx, norm_ffn[l]), cmod[3], cmod[4])
            tokens = jnp.concatenate([tokens, hc2.reshape(B * Lc, D)], axis=0)
        f = hier_moe(tokens, moe_w_group[l], moe_b_group[l], moe_w_expert[l], moe_b_expert[l],
                     moe_w1[l], moe_w3[l], moe_w2[l])
        x = x + mod[5] * f[:B * S].reshape(B, S, D)
        if ctx_out:
            ctx = ctx + cmod[5] * f[B * S:].reshape(B, Lc, D)
    return rms_norm(x, norm_final)
```

```python
import functools
import math

import jax
import jax.numpy as jnp
import numpy as np
from jax import lax
from jax.experimental import pallas as pl
from jax.experimental.pallas import tpu as pltpu

F32 = jnp.float32
BF16 = jnp.bfloat16

NORM_EPS = 1e-6
NEG_INF = -1e30
ROPE_THETA = 10000.0
GRID_W = 64
HEAD_DIM = 128
NA_WIN_ROWS = 8
NA_WIN_COLS = 16
GLA_HEADS = 4
GLA_DK = 128
GLA_LOWRANK = 16
GLA_GATE_NORMALIZER = 16.0
GLA_CHUNK = 64
HY_ORDER = 2
HY_EMB = 33
HY_SIN_FREQ = 1.0
HY_DECAY_TARGET = 1e-2
HY_FAST_DECAY = 0.3
HY_SLOW_DECAY = 1.5
CONV_W = 3
N_GROUPS = 4
EXPERTS_PER_GROUP = 8
N_EXPERTS = N_GROUPS * EXPERTS_PER_GROUP
TOP_K_IN_GROUP = 2

ROW_TILE = 256
NA_GROUP_ROWS = 4
NA_SPAN_ROWS = NA_GROUP_ROWS + NA_WIN_ROWS
MOE_TILE = 256
VMEM_LIMIT = 56 * 1024 * 1024


def _params(*sem):
    return pltpu.CompilerParams(dimension_semantics=sem, vmem_limit_bytes=VMEM_LIMIT)


def _dot(a, b):
    return jnp.dot(a.astype(BF16), b.astype(BF16), preferred_element_type=F32)


def _dot_nt(a, b):
    return lax.dot_general(a.astype(BF16), b.astype(BF16), (((1,), (1,)), ((), ())),
                           preferred_element_type=F32)


def _mm_kernel(a_ref, w_ref, o_ref):
    o_ref[...] = _dot(a_ref[...], w_ref[...]).astype(o_ref.dtype)


def mm(a, w, out_dtype, tm, tn):
    M, K = a.shape
    N = w.shape[1]
    assert M % tm == 0 and N % tn == 0, (M, N, tm, tn)
    return pl.pallas_call(
        _mm_kernel,
        grid=(N // tn, M // tm),
        in_specs=[pl.BlockSpec((tm, K), lambda j, i: (i, 0)),
                  pl.BlockSpec((K, tn), lambda j, i: (0, j))],
        out_specs=pl.BlockSpec((tm, tn), lambda j, i: (i, j)),
        out_shape=jax.ShapeDtypeStruct((M, N), out_dtype),
        compiler_params=_params("arbitrary", "arbitrary"),
        name="mm",
    )(a, w)


def _normmod_kernel(x_ref, g_ref, mod_ref, o_ref, *, shift_row, scale_row):
    x = x_ref[0]
    y = x * lax.rsqrt(jnp.mean(x * x, axis=-1, keepdims=True) + NORM_EPS) * g_ref[...]
    mod = mod_ref[0, 0]
    y = y * (1.0 + mod[scale_row:scale_row + 1]) + mod[shift_row:shift_row + 1]
    o_ref[0] = y.astype(o_ref.dtype)


def normmod(xs, gain, modtab, shift_row, scale_row, S, out_dtype):
    B, T, D = xs.shape
    n_lat = S // ROW_TILE
    return pl.pallas_call(
        functools.partial(_normmod_kernel, shift_row=shift_row, scale_row=scale_row),
        grid=(B, T // ROW_TILE),
        in_specs=[pl.BlockSpec((1, ROW_TILE, D), lambda b, i: (b, i, 0)),
                  pl.BlockSpec((1, D), lambda b, i: (0, 0)),
                  pl.BlockSpec((1, 1, 8, D), lambda b, i: (b, jnp.where(i >= n_lat, 1, 0), 0, 0))],
        out_specs=pl.BlockSpec((1, ROW_TILE, D), lambda b, i: (b, i, 0)),
        out_shape=jax.ShapeDtypeStruct((B, T, D), out_dtype),
        compiler_params=_params("arbitrary", "arbitrary"),
        name="normmod",
    )(xs, gain.reshape(1, D), modtab)


def _mm_res_kernel(*refs, n_a, gate_row):
    a_refs = refs[:n_a]
    w_refs = refs[n_a:2 * n_a]
    res_ref, mod_ref, o_ref = refs[2 * n_a:]
    acc = _dot(a_refs[0][0], w_refs[0][...])
    for a_ref, w_ref in zip(a_refs[1:], w_refs[1:]):
        acc = acc + _dot(a_ref[0], w_ref[...])
    gate = mod_ref[0, 0][gate_row:gate_row + 1]
    o_ref[0] = res_ref[0] + gate * acc


def mm_res(a_list, w_list, res, modtab, gate_row, S):
    B, T, D = res.shape
    n_lat = S // ROW_TILE
    n_a = len(a_list)
    in_specs = [pl.BlockSpec((1, ROW_TILE, a.shape[-1]), lambda b, i: (b, i, 0)) for a in a_list]
    in_specs += [pl.BlockSpec(w.shape, lambda b, i: (0, 0)) for w in w_list]
    in_specs += [pl.BlockSpec((1, ROW_TILE, D), lambda b, i: (b, i, 0)),
                 pl.BlockSpec((1, 1, 8, D), lambda b, i: (b, jnp.where(i >= n_lat, 1, 0), 0, 0))]
    return pl.pallas_call(
        functools.partial(_mm_res_kernel, n_a=n_a, gate_row=gate_row),
        grid=(B, T // ROW_TILE),
        in_specs=in_specs,
        out_specs=pl.BlockSpec((1, ROW_TILE, D), lambda b, i: (b, i, 0)),
        out_shape=jax.ShapeDtypeStruct((B, T, D), F32),
        compiler_params=_params("arbitrary", "arbitrary"),
        name="mm_res",
    )(*a_list, *w_list, res, modtab)


def _na_kernel(q_ref, k_ref, v_ref, bias_ref, o_ref, *, S, Lc, n_lat):
    i = pl.program_id(2)
    rows = S // GRID_W
    scale = HEAD_DIM ** -0.5
    q = q_ref[0]
    kc = k_ref[0, pl.ds(S, Lc), :]
    vc = v_ref[0, pl.ds(S, Lc), :]
    s_ctx = _dot_nt(q, kc) * scale

    @pl.when(i < n_lat)
    def _():
        ks = jnp.clip(NA_GROUP_ROWS * i - NA_WIN_ROWS // 2, 0, rows - NA_SPAN_ROWS)
        start = pl.multiple_of(ks * GRID_W, GRID_W)
        kl = k_ref[0, pl.ds(start, NA_SPAN_ROWS * GRID_W), :]
        vl = v_ref[0, pl.ds(start, NA_SPAN_ROWS * GRID_W), :]
        s_loc = _dot_nt(q, kl) * scale + bias_ref[0, 0]
        m = jnp.maximum(jnp.max(s_loc, axis=-1, keepdims=True), jnp.max(s_ctx, axis=-1, keepdims=True))
        p_loc = jnp.exp(s_loc - m)
        p_ctx = jnp.exp(s_ctx - m)
        den = jnp.sum(p_loc, axis=-1, keepdims=True) + jnp.sum(p_ctx, axis=-1, keepdims=True)
        o = _dot(p_loc, vl) + _dot(p_ctx, vc)
        o_ref[0] = (o / den).astype(o_ref.dtype)

    @pl.when(i >= n_lat)
    def _():
        m = jnp.max(s_ctx, axis=-1, keepdims=True)
        p = jnp.exp(s_ctx - m)
        o = _dot(p, vc)
        o_ref[0] = (o / jnp.sum(p, axis=-1, keepdims=True)).astype(o_ref.dtype)


def na_bias_table(rpb, S):
    rows = S // GRID_W
    n_lat = S // ROW_TILE
    kc = NA_WIN_COLS
    cols = jnp.arange(GRID_W)
    col_start = jnp.clip(cols - kc // 2, 0, GRID_W - kc)
    col_ok = (cols[None, :] >= col_start[:, None]) & (cols[None, :] < col_start[:, None] + kc)
    coff = jnp.clip(cols[None, :] - cols[:, None], -(kc - 1), kc - 1) + kc - 1
    cb = jnp.take(rpb.astype(F32), coff, axis=-1)
    cb = jnp.where(col_ok[None, None], cb, NEG_INF)
    g = jnp.array([0, min(1, n_lat - 1), n_lat - 1])
    ks = jnp.clip(NA_GROUP_ROWS * g - NA_WIN_ROWS // 2, 0, rows - NA_SPAN_ROWS)
    r = NA_GROUP_ROWS * g[:, None] + jnp.arange(NA_GROUP_ROWS)[None, :]
    k = ks[:, None] + jnp.arange(NA_SPAN_ROWS)[None, :]
    rs = jnp.clip(r - NA_WIN_ROWS // 2, 0, rows - NA_WIN_ROWS)
    valid = (k[:, None, :] >= rs[:, :, None]) & (k[:, None, :] < rs[:, :, None] + NA_WIN_ROWS)
    delta = jnp.clip(k[:, None, :] - r[:, :, None] + NA_WIN_ROWS - 1, 0, 2 * NA_WIN_ROWS - 2)
    t = cb[:, delta]
    t = jnp.where(valid[None, :, :, :, None, None], t, NEG_INF)
    t = t.transpose(0, 1, 2, 4, 3, 5)
    H = rpb.shape[0]
    return t.reshape(H, 3, NA_GROUP_ROWS * GRID_W, NA_SPAN_ROWS * GRID_W)


def na_attention(qkv, bias_tab, S):
    B, T, W3 = qkv.shape
    H = W3 // (3 * HEAD_DIM)
    Lc = T - S
    n_lat = S // ROW_TILE
    assert Lc == ROW_TILE and NA_GROUP_ROWS * GRID_W == ROW_TILE
    span = NA_SPAN_ROWS * GRID_W

    def bias_idx(b, h, i):
        return (h, jnp.where(i == 0, 0, jnp.where(i == n_lat - 1, 2, 1)), 0, 0)

    return pl.pallas_call(
        functools.partial(_na_kernel, S=S, Lc=Lc, n_lat=n_lat),
        grid=(B, H, n_lat + 1),
        in_specs=[pl.BlockSpec((1, ROW_TILE, HEAD_DIM), lambda b, h, i: (b, i, h)),
                  pl.BlockSpec((1, T, HEAD_DIM), lambda b, h, i: (b, 0, H + h)),
                  pl.BlockSpec((1, T, HEAD_DIM), lambda b, h, i: (b, 0, 2 * H + h)),
                  pl.BlockSpec((1, 1, ROW_TILE, span), bias_idx)],
        out_specs=pl.BlockSpec((1, ROW_TILE, HEAD_DIM), lambda b, h, i: (b, i, h)),
        out_shape=jax.ShapeDtypeStruct((B, T, H * HEAD_DIM), BF16),
        compiler_params=_params("arbitrary", "arbitrary", "arbitrary"),
        name="na_attention",
    )(qkv, qkv, qkv, bias_tab)


GLA_HEAD_COLS = 4 * GLA_DK + 2 * GLA_DK


def _gla_chunk(q, k, v, la, state_ref, reverse):
    C = GLA_CHUNK
    ri = lax.broadcasted_iota(jnp.int32, (C, C), 0)
    ci = lax.broadcasted_iota(jnp.int32, (C, C), 1)
    keep = (ci >= ri) if reverse else (ci <= ri)
    tri = keep.astype(F32)
    b = jnp.dot(tri, la, preferred_element_type=F32, precision=lax.Precision.HIGHEST)
    mid = C // 2 if reverse else C // 2 - 1
    last = 0 if reverse else C - 1
    b_mid = b[mid:mid + 1]
    b_last = b[last:last + 1]
    att = _dot_nt(q * jnp.exp(b - b_mid), k * jnp.exp(b_mid - b))
    att = jnp.where(keep, att, 0.0)
    s = state_ref[...]
    o = _dot(att, v) + _dot(q * jnp.exp(b), s)
    kd_t = (k * jnp.exp(b_last - b)).T
    dec = jnp.exp(jnp.broadcast_to(b_last, (GLA_DK, GLA_DK)).T[:, :1])
    state_ref[...] = dec * s + _dot(kd_t, v)
    return o


def _gla_kernel(x_ref, lr_ref, cos_ref, sin_ref, up_ref, gb_ref, o_ref, state_ref, *, n_blk):
    d = pl.program_id(2)
    i = pl.program_id(3)
    C = GLA_CHUNK
    dk = GLA_DK

    @pl.when(i == 0)
    def _():
        state_ref[...] = jnp.zeros_like(state_ref)

    def run(reverse):
        dirn = 1 if reverse else 0
        chunks = range(ROW_TILE // C - 1, -1, -1) if reverse else range(ROW_TILE // C)
        up = up_ref[0, dirn]
        gb = gb_ref[0, dirn]
        for c in chunks:
            sl = slice(c * C, (c + 1) * C)
            x = x_ref[0, sl, :]
            cos = cos_ref[sl, :]
            sin = sin_ref[sl, :]
            q = (x[:, 0:dk] * cos + x[:, dk:2 * dk] * sin) * (dk ** -0.5)
            k = x[:, 2 * dk:3 * dk] * cos + x[:, 3 * dk:4 * dk] * sin
            v = x[:, 4 * dk:6 * dk]
            z = jnp.dot(lr_ref[0, sl, :], up, preferred_element_type=F32, precision=lax.Precision.HIGHEST) + gb
            la = (jnp.minimum(z, 0.0) - jnp.log(1.0 + jnp.exp(-jnp.abs(z)))) / GLA_GATE_NORMALIZER
            o_ref[0, 0, sl, :] = _gla_chunk(q, k, v, la, state_ref, reverse)

    @pl.when(d == 0)
    def _():
        run(False)

    @pl.when(d == 1)
    def _():
        run(True)


def gla_scan(gl, lr, cos_tab, sin_tab, up_tied, bias_tied, S):
    B, T, W = gl.shape
    H = W // GLA_HEAD_COLS
    n_blk = T // ROW_TILE
    dv = 2 * GLA_DK

    def blk(d, i):
        return jnp.where(i == 0, n_blk - 1, jnp.where(d == 0, i - 1, n_blk - 1 - i))

    return pl.pallas_call(
        functools.partial(_gla_kernel, n_blk=n_blk),
        grid=(B, H, 2, n_blk),
        in_specs=[pl.BlockSpec((1, ROW_TILE, GLA_HEAD_COLS), lambda b, h, d, i: (b, blk(d, i), h)),
                  pl.BlockSpec((1, ROW_TILE, 128), lambda b, h, d, i: (b, blk(d, i), 0)),
                  pl.BlockSpec((ROW_TILE, GLA_DK), lambda b, h, d, i: (blk(d, i), 0)),
                  pl.BlockSpec((ROW_TILE, GLA_DK), lambda b, h, d, i: (blk(d, i), 0)),
                  pl.BlockSpec((1, 2, 128, GLA_DK), lambda b, h, d, i: (h, 0, 0, 0)),
                  pl.BlockSpec((1, 2, 1, GLA_DK), lambda b, h, d, i: (h, 0, 0, 0))],
        out_specs=pl.BlockSpec((1, 1, ROW_TILE, dv), lambda b, h, d, i: (d, b, blk(d, i), h)),
        out_shape=jax.ShapeDtypeStruct((2, B, T, H * dv), F32),
        scratch_shapes=[pltpu.VMEM((GLA_DK, dv), F32)],
        compiler_params=_params("arbitrary", "arbitrary", "arbitrary", "arbitrary"),
        name="gla_scan",
    )(gl, lr, cos_tab, sin_tab, up_tied, bias_tied)


def _gla_finish_kernel(o_ref, g_ref, gn_ref, y_ref, *, H, dv):
    gn = gn_ref[...]
    for h in range(H):
        sl = slice(h * dv, (h + 1) * dv)
        o = o_ref[0, 0, :, sl] + o_ref[1, 0, :, sl]
        o = o * lax.rsqrt(jnp.mean(o * o, axis=-1, keepdims=True) + NORM_EPS) * gn
        g = g_ref[0, :, sl]
        y_ref[0, :, sl] = (o * (g * jax.nn.sigmoid(g))).astype(y_ref.dtype)


def gla_finish(o2, g, gnorm):
    _, B, T, W = o2.shape
    dv = gnorm.shape[0]
    H = W // dv
    return pl.pallas_call(
        functools.partial(_gla_finish_kernel, H=H, dv=dv),
        grid=(B, T // ROW_TILE),
        in_specs=[pl.BlockSpec((2, 1, ROW_TILE, W), lambda b, i: (0, b, i, 0)),
                  pl.BlockSpec((1, ROW_TILE, W), lambda b, i: (b, i, 0)),
                  pl.BlockSpec((1, dv), lambda b, i: (0, 0))],
        out_specs=pl.BlockSpec((1, ROW_TILE, W), lambda b, i: (b, i, 0)),
        out_shape=jax.ShapeDtypeStruct((B, T, W), BF16),
        compiler_params=_params("arbitrary", "arbitrary"),
        name="gla_finish",
    )(o2, g, gnorm.reshape(1, dv))


def rope_tables(S, T):
    half = GLA_DK // 2
    nf = half // 2
    pos = jnp.arange(S)
    inv_freq = ROPE_THETA ** (-jnp.arange(nf, dtype=F32) / nf)

    def part(p):
        ang = p.astype(F32)[:, None] * inv_freq[None, :]
        c, s = jnp.cos(ang), jnp.sin(ang)
        return jnp.concatenate([c, c], axis=-1), jnp.concatenate([-s, s], axis=-1)

    cr, sr = part(pos // GRID_W)
    cc, sc = part(pos % GRID_W)
    cos = jnp.concatenate([cr, cc], axis=-1)
    sin = jnp.concatenate([sr, sc], axis=-1)
    cos = jnp.concatenate([cos, jnp.ones((T - S, GLA_DK), F32)], axis=0)
    sin = jnp.concatenate([sin, jnp.zeros((T - S, GLA_DK), F32)], axis=0)
    return cos, sin


def _rot_partner_cols(w):
    K = w.shape[0]
    w4 = w.reshape(K, GLA_HEADS, 4, GLA_DK // 4)
    return w4[:, :, jnp.array([1, 0, 3, 2])].reshape(K, GLA_HEADS * GLA_DK)


def _tie_pairs(u):
    lead = u.shape[:-1]
    u = u.reshape(lead + (GLA_HEADS, 2, 1, GLA_DK // 4))
    u = jnp.broadcast_to(u, lead + (GLA_HEADS, 2, 2, GLA_DK // 4))
    return u.reshape(lead + (GLA_HEADS, GLA_DK))


def _moe_kernel(be_ref, bv_ref, x_ref, w1_ref, w3_ref, w2_ref, g_ref, o_ref, w1b, w3b, w2b):
    i = pl.program_id(0)
    changed = jnp.logical_or(i == 0, be_ref[i] != be_ref[jnp.maximum(i - 1, 0)])

    @pl.when(changed)
    def _():
        w1b[...] = w1_ref[0, 0].astype(BF16)
        w3b[...] = w3_ref[0, 0].astype(BF16)
        w2b[...] = w2_ref[0, 0].astype(BF16)

    @pl.when(bv_ref[i] != 0)
    def _():
        x = x_ref[...]
        h1 = jnp.dot(x, w1b[...], preferred_element_type=F32)
        h3 = jnp.dot(x, w3b[...], preferred_element_type=F32)
        a = (h1 * jax.nn.sigmoid(h1)) * h3
        y = jnp.dot(a.astype(BF16), w2b[...], preferred_element_type=F32)
        o_ref[...] = y * g_ref[...]

    @pl.when(bv_ref[i] == 0)
    def _():
        o_ref[...] = jnp.zeros_like(o_ref)


def moe_experts(xs, layer, w1, w3, w2, slot_gate, block_expert, block_valid):
    cap, D = xs.shape
    De = w1.shape[-1]
    n_blocks = cap // MOE_TILE
    return pl.pallas_call(
        _moe_kernel,
        grid_spec=pltpu.PrefetchScalarGridSpec(
            num_scalar_prefetch=2,
            grid=(n_blocks,),
            in_specs=[pl.BlockSpec((MOE_TILE, D), lambda i, be, bv: (i, 0)),
                      pl.BlockSpec((1, 1, D, De), lambda i, be, bv: (layer, be[i], 0, 0)),
                      pl.BlockSpec((1, 1, D, De), lambda i, be, bv: (layer, be[i], 0, 0)),
                      pl.BlockSpec((1, 1, De, D), lambda i, be, bv: (layer, be[i], 0, 0)),
                      pl.BlockSpec((MOE_TILE, 1), lambda i, be, bv: (i, 0))],
            out_specs=pl.BlockSpec((MOE_TILE, D), lambda i, be, bv: (i, 0)),
            scratch_shapes=[pltpu.VMEM((D, De), BF16), pltpu.VMEM((D, De), BF16), pltpu.VMEM((De, D), BF16)]),
        out_shape=jax.ShapeDtypeStruct((cap, D), F32),
        compiler_params=_params("arbitrary"),
        name="moe_experts",
    )(block_expert, block_valid, xs, w1, w3, w2, slot_gate)


def hier_moe(tok, w_router, b_router, layer, w1, w3, w2):
    N, D = tok.shape
    logits = mm(tok, w_router, F32, 1024 if N % 1024 == 0 else ROW_TILE, 128)[:, :N_GROUPS + N_EXPERTS] + b_router
    lg = logits[:, :N_GROUPS]
    grp = jnp.argmax(lg, axis=-1)
    p_grp = jnp.take_along_axis(jax.nn.softmax(lg, axis=-1), grp[:, None], axis=-1)
    le = logits[:, N_GROUPS:].reshape(N, N_GROUPS, EXPERTS_PER_GROUP)
    le = jnp.take_along_axis(le, grp[:, None, None], axis=1)[:, 0]
    top_v, top_i = lax.top_k(le, TOP_K_IN_GROUP)
    gate = p_grp * jax.nn.softmax(top_v, axis=-1)
    eid = (grp[:, None] * EXPERTS_PER_GROUP + top_i).astype(jnp.int32)
    A = N * TOP_K_IN_GROUP
    e_flat = eid.reshape(A)
    onehot = (e_flat[:, None] == jnp.arange(N_EXPERTS, dtype=jnp.int32)[None, :]).astype(jnp.int32)
    csum = jnp.cumsum(onehot, axis=0)
    rank = jnp.take_along_axis(csum, e_flat[:, None], axis=1)[:, 0] - 1
    counts = csum[-1]
    padded = (counts + MOE_TILE - 1) // MOE_TILE * MOE_TILE
    pend = jnp.cumsum(padded)
    pstart = pend - padded
    slot = pstart[e_flat] + rank
    n_blocks = -(-A // MOE_TILE) + N_EXPERTS
    cap = n_blocks * MOE_TILE
    slot_tok = jnp.full((cap,), N, jnp.int32).at[slot].set(jnp.arange(A, dtype=jnp.int32) // TOP_K_IN_GROUP)
    slot_gate = jnp.zeros((cap,), F32).at[slot].set(gate.reshape(A))
    bstart = jnp.arange(n_blocks, dtype=jnp.int32) * MOE_TILE
    block_expert = jnp.minimum(jnp.searchsorted(pend, bstart, side='right'), N_EXPERTS - 1).astype(jnp.int32)
    block_valid = (bstart < pend[-1]).astype(jnp.int32)
    tok_pad = jnp.concatenate([tok, jnp.zeros((1, D), tok.dtype)], axis=0)
    xs = tok_pad[slot_tok]
    ys = moe_experts(xs, layer, w1, w3, w2, slot_gate[:, None], block_expert, block_valid)
    pair = ys[slot.reshape(N, TOP_K_IN_GROUP)]
    return pair[:, 0] + pair[:, 1]


def _dwconv(u, w):
    return lax.conv_general_dilated(
        u, w.astype(u.dtype)[:, None, :], window_strides=(1,), padding=[(CONV_W // 2, CONV_W // 2)],
        dimension_numbers=('NWC', 'WIO', 'NWC'), feature_group_count=u.shape[-1])


def _hyena_filters(L, C, w1, b1, w2, b2, w3, b3, w4):
    hp = lax.Precision.HIGHEST
    t = jnp.linspace(0.0, 1.0, L, dtype=F32)[:, None]
    bands = (HY_EMB - 1) // 2
    freqs = jnp.linspace(1e-4, bands - 1, bands, dtype=F32)[None, :]
    w = (2.0 * math.pi / L) * jnp.arange(L, dtype=F32)[:, None]
    emb = jnp.concatenate([t, jnp.cos(freqs * w), -jnp.sin(freqs * w)], axis=-1)
    a = jnp.sin(HY_SIN_FREQ * (jnp.dot(emb, w1, precision=hp) + b1))
    a = jnp.sin(HY_SIN_FREQ * (jnp.dot(a, w2, precision=hp) + b2))
    a = jnp.sin(HY_SIN_FREQ * (jnp.dot(a, w3, precision=hp) + b3))
    h = jnp.dot(a, w4, precision=hp).reshape(L, HY_ORDER, 2, C)
    deltas = jnp.abs(jnp.linspace(math.log(HY_DECAY_TARGET) / HY_SLOW_DECAY,
                                  math.log(HY_DECAY_TARGET) / HY_FAST_DECAY, C, dtype=F32))
    h = h * jnp.exp(-t * deltas)[:, None, None, :]
    return h / (jnp.sum(jnp.abs(h), axis=(0, 2), keepdims=True) + NORM_EPS)


def _long_conv_fft(z, h_fwd, h_bwd, bias):
    L = z.shape[1]
    n = 2 * L
    Z = jnp.fft.rfft(z, n=n, axis=1)
    Hs = jnp.fft.rfft(h_fwd, n=n, axis=0) + jnp.conj(jnp.fft.rfft(h_bwd, n=n, axis=0))
    y = jnp.fft.irfft(Z * Hs[None], n=n, axis=1)[:, :L]
    return y + z * bias


def odd_mix_seq(p, hy_short, hyp, hy_bias, sc_conv):
    C = sc_conv.shape[-1]
    L = p.shape[1]
    n_hy = (HY_ORDER + 1) * C
    v, *gates = jnp.split(_dwconv(p[..., :n_hy], hy_short), HY_ORDER + 1, axis=-1)
    filt = _hyena_filters(L, C, *hyp)
    z = v
    for n, gate in enumerate(gates):
        z = gate * _long_conv_fft(z, filt[:, n, 0], filt[:, n, 1], hy_bias[n])
    b_gate, c_gate, x_in = jnp.split(p[..., n_hy:], 3, axis=-1)
    y_sc = b_gate * _dwconv(c_gate * x_in, sc_conv)
    return jnp.concatenate([z, y_sc], axis=-1)


def _flat(a):
    return a.reshape(a.shape[0] * a.shape[1], a.shape[2])


def _proj(h, w, out_dtype, tn):
    B, T, D = h.shape
    M = B * T
    tm = 1024 if M % 1024 == 0 else ROW_TILE
    return mm(_flat(h), w, out_dtype, tm, tn).reshape(B, T, w.shape[1])


def even_layer(h, xs, modtab, S, w_in, w_out, rpb, gk_up, gk_bias, gnorm):
    B, T, D = xs.shape
    na_w = w_in.shape[1] - 2 * GLA_HEADS * GLA_DK - 2 * (D // 2) - 2 * GLA_LOWRANK
    na_w //= 3
    qk_w = GLA_HEADS * GLA_DK
    v_w = D // 2
    dv = v_w // GLA_HEADS
    o = 3 * na_w
    w_na = w_in[:, :o].astype(BF16)
    wq, wk = w_in[:, o:o + qk_w], w_in[:, o + qk_w:o + 2 * qk_w]
    wv = w_in[:, o + 2 * qk_w:o + 2 * qk_w + v_w]
    wg = w_in[:, o + 2 * qk_w + v_w:o + 2 * qk_w + 2 * v_w]
    wlr = w_in[:, o + 2 * qk_w + 2 * v_w:]
    per_head = lambda w, width: w.reshape(D, GLA_HEADS, width)
    w_gl = jnp.concatenate([per_head(wq, GLA_DK), per_head(_rot_partner_cols(wq), GLA_DK),
                            per_head(wk, GLA_DK), per_head(_rot_partner_cols(wk), GLA_DK),
                            per_head(wv, dv)], axis=-1).reshape(D, GLA_HEADS * GLA_HEAD_COLS).astype(BF16)
    w_g = wg.astype(BF16)
    w_lr = jnp.pad(wlr, ((0, 0), (0, 128 - 2 * GLA_LOWRANK))).astype(BF16)

    qkv = _proj(h, w_na, BF16, 1024)
    gl = _proj(h, w_gl, F32, 768)
    g = _proj(h, w_g, F32, 1024)
    lr = _proj(h, w_lr, F32, 128)

    ya = na_attention(qkv, na_bias_table(rpb, S), S)
    cos_tab, sin_tab = rope_tables(S, T)
    up_tied = _tie_pairs(gk_up.astype(F32)).transpose(2, 0, 1, 3)
    up_tied = jnp.stack([jnp.pad(up_tied[:, 0], ((0, 0), (0, 128 - GLA_LOWRANK), (0, 0))),
                         jnp.pad(up_tied[:, 1], ((0, 0), (GLA_LOWRANK, 128 - 2 * GLA_LOWRANK), (0, 0)))],
                        axis=1)
    bias_tied = _tie_pairs(gk_bias.astype(F32)).transpose(1, 0, 2)[:, :, None, :]
    o2 = gla_scan(gl, lr, cos_tab, sin_tab, up_tied, bias_tied, S)
    yb = gla_finish(o2, g, gnorm)
    w_out_b = w_out.astype(BF16)
    return mm_res([ya, yb], [w_out_b[:na_w * 1], w_out_b[na_w:]], xs, modtab, 2, S)


def odd_layer(h, xs, modtab, S, ctx_live, w_in, w_out, hy_short, hyp, hy_bias, sc_conv):
    p = _proj(h, w_in.astype(BF16), F32, 1024)
    y_lat = odd_mix_seq(p[:, :S], hy_short, hyp, hy_bias, sc_conv)
    if ctx_live:
        y_ctx = odd_mix_seq(p[:, S:], hy_short, hyp, hy_bias, sc_conv)
    else:
        y_ctx = jnp.zeros((p.shape[0], p.shape[1] - S, y_lat.shape[-1]), F32)
    y = jnp.concatenate([y_lat, y_ctx], axis=1).astype(BF16)
    return mm_res([y], [w_out.astype(BF16)], xs, modtab, 2, S)


def kernel(x, c, ctx, c_ctx, w_mod, b_mod, norm_mix, norm_ffn, norm_final, w_in_even, w_out_even, na_rpb,
           gla_gk_up, gla_gk_bias, gla_norm, w_in_odd, w_out_odd, hy_short, hy_w1, hy_b1, hy_w2, hy_b2,
           hy_w3, hy_b3, hy_w4, hy_bias, sc_conv, moe_w_group, moe_b_group, moe_w_expert, moe_b_expert,
           moe_w1, moe_w3, moe_w2):
    B, S, D = x.shape
    Lc = ctx.shape[1]
    T = S + Lc
    depth = w_mod.shape[0]
    last_even = 2 * ((depth - 1) // 2)
    xs = jnp.concatenate([x, ctx], axis=1)

    cvec = jnp.concatenate([c, c_ctx[None, :], jnp.zeros((8 - B - 1, D), F32)], axis=0)
    cvec = cvec * jax.nn.sigmoid(cvec)
    for l in range(depth):
        even = (l % 2 == 0)
        ctx_live = l < last_even
        mod = mm(cvec, w_mod[l], F32, 8, 1024)[:B + 1] + b_mod[l]
        mod = mod.reshape(B + 1, 6, D)
        lat = mod[:B]
        cx = jnp.broadcast_to(mod[B][None], (B, 6, D))
        modtab = jnp.pad(jnp.stack([lat, cx], axis=1), ((0, 0), (0, 0), (0, 2), (0, 0)))

        h = normmod(xs, norm_mix[l], modtab, 0, 1, S, BF16)
        if even:
            e = l // 2
            xs = even_layer(h, xs, modtab, S, w_in_even[e], w_out_even[e], na_rpb[e], gla_gk_up[e],
                            gla_gk_bias[e], gla_norm[e])
        else:
            o = l // 2
            hyp = (hy_w1[o], hy_b1[o], hy_w2[o], hy_b2[o], hy_w3[o], hy_b3[o], hy_w4[o])
            xs = odd_layer(h, xs, modtab, S, ctx_live, w_in_odd[o], w_out_odd[o], hy_short[o], hyp,
                           hy_bias[o], sc_conv[o])

        tok = normmod(xs, norm_ffn[l], modtab, 3, 4, S, BF16)
        w_router = jnp.pad(jnp.concatenate([moe_w_group[l], moe_w_expert[l]], axis=1),
                           ((0, 0), (0, 128 - N_GROUPS - N_EXPERTS))).astype(BF16)
        b_router = jnp.concatenate([moe_b_group[l], moe_b_expert[l]])
        f = hier_moe(_flat(tok), w_router, b_router, l, moe_w1, moe_w3, moe_w2).reshape(B, T, D)
        gate = jnp.concatenate([jnp.broadcast_to(lat[:, None, 5], (B, S, D)),
                                jnp.broadcast_to(cx[:, None, 5], (B, Lc, D))], axis=1)
        xs = xs + gate * f

    zero_tab = jnp.zeros((B, 2, 8, D), F32)
    out = normmod(xs, norm_final, zero_tab, 0, 1, S, F32)
    return out[:, :S]
```

```python
import functools
import math

import jax
import jax.numpy as jnp
import numpy as np
from jax import lax
from jax.experimental import pallas as pl
from jax.experimental.pallas import tpu as pltpu

F32 = jnp.float32
BF16 = jnp.bfloat16

NORM_EPS = 1e-6
NEG_INF = -1e30
ROPE_THETA = 10000.0
GRID_W = 64
HEAD_DIM = 128
NA_WIN_ROWS = 8
NA_WIN_COLS = 16
GLA_HEADS = 4
GLA_DK = 128
GLA_LOWRANK = 16
GLA_GATE_NORMALIZER = 16.0
GLA_CHUNK = 64
HY_ORDER = 2
HY_EMB = 33
HY_SIN_FREQ = 1.0
HY_DECAY_TARGET = 1e-2
HY_FAST_DECAY = 0.3
HY_SLOW_DECAY = 1.5
CONV_W = 3
N_GROUPS = 4
EXPERTS_PER_GROUP = 8
N_EXPERTS = N_GROUPS * EXPERTS_PER_GROUP
TOP_K_IN_GROUP = 2

ROW_TILE = 256
NA_GROUP_ROWS = 4
NA_SPAN_ROWS = NA_GROUP_ROWS + NA_WIN_ROWS
MOE_TILE = 256
VMEM_LIMIT = 56 * 1024 * 1024


def _params(*sem):
    return pltpu.CompilerParams(dimension_semantics=sem, vmem_limit_bytes=VMEM_LIMIT)


def _dot(a, b):
    return jnp.dot(a.astype(BF16), b.astype(BF16), preferred_element_type=F32)


def _dot_nt(a, b):
    return lax.dot_general(a.astype(BF16), b.astype(BF16), (((1,), (1,)), ((), ())),
                           preferred_element_type=F32)


def _mm_kernel(a_ref, w_ref, o_ref):
    o_ref[...] = _dot(a_ref[...], w_ref[...]).astype(o_ref.dtype)


def mm(a, w, out_dtype, tm, tn):
    M, K = a.shape
    N = w.shape[1]
    assert M % tm == 0 and N % tn == 0, (M, N, tm, tn)
    return pl.pallas_call(
        _mm_kernel,
        grid=(N // tn, M // tm),
        in_specs=[pl.BlockSpec((tm, K), lambda j, i: (i, 0)),
                  pl.BlockSpec((K, tn), lambda j, i: (0, j))],
        out_specs=pl.BlockSpec((tm, tn), lambda j, i: (i, j)),
        out_shape=jax.ShapeDtypeStruct((M, N), out_dtype),
        compiler_params=_params("arbitrary", "arbitrary"),
        name="mm",
    )(a, w)


MOD_MIX = (0, 1, 2)
MOD_FFN = (3, 4, 5)


def _norm_modulate(x, gain, mod, rows):
    y = x * lax.rsqrt(jnp.mean(x * x, axis=-1, keepdims=True) + NORM_EPS) * gain
    return y * (1.0 + mod[rows[1]:rows[1] + 1]) + mod[rows[0]:rows[0] + 1]


def _normmod_kernel(x_ref, g_ref, mod_ref, o_ref, *, rows):
    o_ref[0] = _norm_modulate(x_ref[0], g_ref[...], mod_ref[0, 0], rows).astype(o_ref.dtype)


def normmod(xs, gain, modtab, rows, S, out_dtype):
    B, T, D = xs.shape
    n_lat = S // ROW_TILE
    return pl.pallas_call(
        functools.partial(_normmod_kernel, rows=rows),
        grid=(B, T // ROW_TILE),
        in_specs=[pl.BlockSpec((1, ROW_TILE, D), lambda b, i: (b, i, 0)),
                  pl.BlockSpec((1, D), lambda b, i: (0, 0)),
                  pl.BlockSpec((1, 1, 8, D), lambda b, i: (b, jnp.where(i >= n_lat, 1, 0), 0, 0))],
        out_specs=pl.BlockSpec((1, ROW_TILE, D), lambda b, i: (b, i, 0)),
        out_shape=jax.ShapeDtypeStruct((B, T, D), out_dtype),
        compiler_params=_params("arbitrary", "arbitrary"),
        name="normmod",
    )(xs, gain.reshape(1, D), modtab)


def _mm_res_kernel(*refs, n_a):
    a_refs = refs[:n_a]
    w_refs = refs[n_a:2 * n_a]
    res_ref, mod_ref, gain_ref, o_ref, t_ref = refs[2 * n_a:]
    acc = _dot(a_refs[0][0], w_refs[0][...])
    for a_ref, w_ref in zip(a_refs[1:], w_refs[1:]):
        acc = acc + _dot(a_ref[0], w_ref[...])
    mod = mod_ref[0, 0]
    x = res_ref[0] + mod[MOD_MIX[2]:MOD_MIX[2] + 1] * acc
    o_ref[0] = x
    t_ref[0] = _norm_modulate(x, gain_ref[...], mod, MOD_FFN).astype(t_ref.dtype)


def mm_res(a_list, w_list, res, modtab, gain_ffn, S):
    B, T, D = res.shape
    n_lat = S // ROW_TILE
    n_a = len(a_list)
    row = pl.BlockSpec((1, ROW_TILE, D), lambda b, i: (b, i, 0))
    in_specs = [pl.BlockSpec((1, ROW_TILE, a.shape[-1]), lambda b, i: (b, i, 0)) for a in a_list]
    in_specs += [pl.BlockSpec(w.shape, lambda b, i: (0, 0)) for w in w_list]
    in_specs += [row, pl.BlockSpec((1, 1, 8, D), lambda b, i: (b, jnp.where(i >= n_lat, 1, 0), 0, 0)),
                 pl.BlockSpec((1, D), lambda b, i: (0, 0))]
    return pl.pallas_call(
        functools.partial(_mm_res_kernel, n_a=n_a),
        grid=(B, T // ROW_TILE),
        in_specs=in_specs,
        out_specs=[row, row],
        out_shape=[jax.ShapeDtypeStruct((B, T, D), F32), jax.ShapeDtypeStruct((B, T, D), BF16)],
        compiler_params=_params("arbitrary", "arbitrary"),
        name="mm_res",
    )(*a_list, *w_list, res, modtab, gain_ffn.reshape(1, D))


def _moe_combine_kernel(x_ref, p_ref, mod_ref, gain_ref, modn_ref, *out_refs, final):
    x = x_ref[0] + mod_ref[0, 0][MOD_FFN[2]:MOD_FFN[2] + 1] * (p_ref[0, 0] + p_ref[1, 0])
    if final:
        out_refs[0][0] = x * lax.rsqrt(jnp.mean(x * x, axis=-1, keepdims=True) + NORM_EPS) * gain_ref[...]
    else:
        out_refs[0][0] = x
        out_refs[1][0] = _norm_modulate(x, gain_ref[...], modn_ref[0, 0], MOD_MIX).astype(out_refs[1].dtype)


def moe_combine(xs, pair, modtab, gain_next, modtab_next, S, final):
    B, T, D = xs.shape
    n_lat = S // ROW_TILE
    rows = S if final else T
    row = pl.BlockSpec((1, ROW_TILE, D), lambda b, i: (b, i, 0))
    mspec = pl.BlockSpec((1, 1, 8, D), lambda b, i: (b, jnp.where(i >= n_lat, 1, 0), 0, 0))
    if final:
        out_specs, out_shape = [row], [jax.ShapeDtypeStruct((B, S, D), F32)]
    else:
        out_specs = [row, row]
        out_shape = [jax.ShapeDtypeStruct((B, T, D), F32), jax.ShapeDtypeStruct((B, T, D), BF16)]
    return pl.pallas_call(
        functools.partial(_moe_combine_kernel, final=final),
        grid=(B, rows // ROW_TILE),
        in_specs=[row, pl.BlockSpec((2, 1, ROW_TILE, D), lambda b, i: (0, b, i, 0)), mspec,
                  pl.BlockSpec((1, D), lambda b, i: (0, 0)), mspec],
        out_specs=out_specs,
        out_shape=out_shape,
        compiler_params=_params("arbitrary", "arbitrary"),
        name="moe_combine",
    )(xs, pair.reshape(2, B, T, D), modtab, gain_next.reshape(1, D), modtab_next)


def _na_kernel(q_ref, k_ref, v_ref, bias_ref, o_ref, *, S, Lc, n_lat):
    i = pl.program_id(2)
    rows = S // GRID_W
    scale = HEAD_DIM ** -0.5
    q = q_ref[0]
    kc = k_ref[0, pl.ds(S, Lc), :]
    vc = v_ref[0, pl.ds(S, Lc), :]
    s_ctx = _dot_nt(q, kc) * scale

    @pl.when(i < n_lat)
    def _():
        ks = jnp.clip(NA_GROUP_ROWS * i - NA_WIN_ROWS // 2, 0, rows - NA_SPAN_ROWS)
        start = pl.multiple_of(ks * GRID_W, GRID_W)
        kl = k_ref[0, pl.ds(start, NA_SPAN_ROWS * GRID_W), :]
        vl = v_ref[0, pl.ds(start, NA_SPAN_ROWS * GRID_W), :]
        s_loc = _dot_nt(q, kl) * scale + bias_ref[0, 0]
        m = jnp.maximum(jnp.max(s_loc, axis=-1, keepdims=True), jnp.max(s_ctx, axis=-1, keepdims=True))
        p_loc = jnp.exp(s_loc - m)
        p_ctx = jnp.exp(s_ctx - m)
        den = jnp.sum(p_loc, axis=-1, keepdims=True) + jnp.sum(p_ctx, axis=-1, keepdims=True)
        o = _dot(p_loc, vl) + _dot(p_ctx, vc)
        o_ref[0] = (o / den).astype(o_ref.dtype)

    @pl.when(i >= n_lat)
    def _():
        m = jnp.max(s_ctx, axis=-1, keepdims=True)
        p = jnp.exp(s_ctx - m)
        o = _dot(p, vc)
        o_ref[0] = (o / jnp.sum(p, axis=-1, keepdims=True)).astype(o_ref.dtype)


def na_bias_table(rpb, S):
    rows = S // GRID_W
    n_lat = S // ROW_TILE
    kc = NA_WIN_COLS
    cols = jnp.arange(GRID_W)
    col_start = jnp.clip(cols - kc // 2, 0, GRID_W - kc)
    col_ok = (cols[None, :] >= col_start[:, None]) & (cols[None, :] < col_start[:, None] + kc)
    coff = jnp.clip(cols[None, :] - cols[:, None], -(kc - 1), kc - 1) + kc - 1
    cb = jnp.take(rpb.astype(F32), coff, axis=-1)
    cb = jnp.where(col_ok[None, None], cb, NEG_INF)
    g = jnp.array([0, min(1, n_lat - 1), n_lat - 1])
    ks = jnp.clip(NA_GROUP_ROWS * g - NA_WIN_ROWS // 2, 0, rows - NA_SPAN_ROWS)
    r = NA_GROUP_ROWS * g[:, None] + jnp.arange(NA_GROUP_ROWS)[None, :]
    k = ks[:, None] + jnp.arange(NA_SPAN_ROWS)[None, :]
    rs = jnp.clip(r - NA_WIN_ROWS // 2, 0, rows - NA_WIN_ROWS)
    valid = (k[:, None, :] >= rs[:, :, None]) & (k[:, None, :] < rs[:, :, None] + NA_WIN_ROWS)
    delta = jnp.clip(k[:, None, :] - r[:, :, None] + NA_WIN_ROWS - 1, 0, 2 * NA_WIN_ROWS - 2)
    t = cb[:, delta]
    t = jnp.where(valid[None, :, :, :, None, None], t, NEG_INF)
    t = t.transpose(0, 1, 2, 4, 3, 5)
    H = rpb.shape[0]
    return t.reshape(H, 3, NA_GROUP_ROWS * GRID_W, NA_SPAN_ROWS * GRID_W)


def na_attention(qkv, bias_tab, S):
    B, T, W3 = qkv.shape
    H = W3 // (3 * HEAD_DIM)
    Lc = T - S
    n_lat = S // ROW_TILE
    assert Lc == ROW_TILE and NA_GROUP_ROWS * GRID_W == ROW_TILE
    span = NA_SPAN_ROWS * GRID_W

    def bias_idx(b, h, i):
        return (h, jnp.where(i == 0, 0, jnp.where(i == n_lat - 1, 2, 1)), 0, 0)

    return pl.pallas_call(
        functools.partial(_na_kernel, S=S, Lc=Lc, n_lat=n_lat),
        grid=(B, H, n_lat + 1),
        in_specs=[pl.BlockSpec((1, ROW_TILE, HEAD_DIM), lambda b, h, i: (b, i, h)),
                  pl.BlockSpec((1, T, HEAD_DIM), lambda b, h, i: (b, 0, H + h)),
                  pl.BlockSpec((1, T, HEAD_DIM), lambda b, h, i: (b, 0, 2 * H + h)),
                  pl.BlockSpec((1, 1, ROW_TILE, span), bias_idx)],
        out_specs=pl.BlockSpec((1, ROW_TILE, HEAD_DIM), lambda b, h, i: (b, i, h)),
        out_shape=jax.ShapeDtypeStruct((B, T, H * HEAD_DIM), BF16),
        compiler_params=_params("arbitrary", "arbitrary", "arbitrary"),
        name="na_attention",
    )(qkv, qkv, qkv, bias_tab)


GLA_HEAD_COLS = 4 * GLA_DK + 2 * GLA_DK


GLA_HEADS_PER_STEP = 2


def _gla_dir(x, lr, cos, sin, up, gb, state, reverse):
    R = x.shape[0]
    C = GLA_CHUNK
    dk = GLA_DK
    hp = lax.Precision.HIGHEST
    q = (x[:, 0:dk] * cos + x[:, dk:2 * dk] * sin) * (dk ** -0.5)
    k = x[:, 2 * dk:3 * dk] * cos + x[:, 3 * dk:4 * dk] * sin
    v = x[:, 4 * dk:6 * dk]
    z = jnp.dot(lr, up, preferred_element_type=F32, precision=hp) + gb
    la = (jnp.minimum(z, 0.0) - jnp.log(1.0 + jnp.exp(-jnp.abs(z)))) / GLA_GATE_NORMALIZER
    ri = lax.broadcasted_iota(jnp.int32, (R, R), 0)
    ci = lax.broadcasted_iota(jnp.int32, (R, R), 1)
    same = (ri // C) == (ci // C)
    cum = jnp.logical_and(same, (ci >= ri) if reverse else (ci <= ri))
    b = jnp.dot(cum.astype(F32), la, preferred_element_type=F32, precision=hp)
    rc = lax.broadcasted_iota(jnp.int32, (C, C), 0)
    cc = lax.broadcasted_iota(jnp.int32, (C, C), 1)
    keep = (cc >= rc) if reverse else (cc <= rc)
    mid = C // 2 if reverse else C // 2 - 1
    last = 0 if reverse else C - 1
    n_chunks = R // C
    outs = [None] * n_chunks
    for c in (range(n_chunks - 1, -1, -1) if reverse else range(n_chunks)):
        sl = slice(c * C, (c + 1) * C)
        bc, qc, kc, vc = b[sl], q[sl], k[sl], v[sl]
        b_mid = bc[mid:mid + 1]
        b_last = bc[last:last + 1]
        att = _dot_nt(qc * jnp.exp(bc - b_mid), kc * jnp.exp(b_mid - bc))
        att = jnp.where(keep, att, 0.0)
        outs[c] = _dot(att, vc) + _dot(qc * jnp.exp(bc), state)
        kd_t = (kc * jnp.exp(b_last - bc)).T
        dec = jnp.exp(jnp.broadcast_to(b_last, (dk, dk)).T[:, :1])
        state = dec * state + _dot(kd_t, vc)
    return jnp.concatenate(outs, axis=0), state


def _gla_kernel(xf_ref, xb_ref, lrf_ref, lrb_ref, cf_ref, sf_ref, cb_ref, sb_ref, up_ref, gb_ref,
                of_ref, ob_ref, state_ref):
    @pl.when(pl.program_id(2) == 0)
    def _():
        state_ref[...] = jnp.zeros_like(state_ref)

    dv = 2 * GLA_DK
    for h in range(GLA_HEADS_PER_STEP):
        cols = slice(h * GLA_HEAD_COLS, (h + 1) * GLA_HEAD_COLS)
        o_f, s_f = _gla_dir(xf_ref[0, :, cols], lrf_ref[0], cf_ref[...], sf_ref[...], up_ref[h, 0], gb_ref[h, 0],
                            state_ref[0, h], False)
        o_b, s_b = _gla_dir(xb_ref[0, :, cols], lrb_ref[0], cb_ref[...], sb_ref[...], up_ref[h, 1], gb_ref[h, 1],
                            state_ref[1, h], True)
        of_ref[0, :, h * dv:(h + 1) * dv] = o_f
        ob_ref[0, :, h * dv:(h + 1) * dv] = o_b
        state_ref[0, h] = s_f
        state_ref[1, h] = s_b


def gla_scan(gl, lr, cos_tab, sin_tab, up_tied, bias_tied, S):
    B, T, W = gl.shape
    H = W // GLA_HEAD_COLS
    n_blk = T // ROW_TILE
    dv = 2 * GLA_DK
    hps = GLA_HEADS_PER_STEP
    blk_f = lambda i: jnp.where(i == 0, n_blk - 1, i - 1)
    blk_b = lambda i: jnp.where(i == 0, n_blk - 1, n_blk - 1 - i)
    xspec = lambda blk: pl.BlockSpec((1, ROW_TILE, hps * GLA_HEAD_COLS), lambda b, h, i: (b, blk(i), h))
    lspec = lambda blk: pl.BlockSpec((1, ROW_TILE, 128), lambda b, h, i: (b, blk(i), 0))
    tspec = lambda blk: pl.BlockSpec((ROW_TILE, GLA_DK), lambda b, h, i: (blk(i), 0))
    ospec = lambda blk: pl.BlockSpec((1, ROW_TILE, hps * dv), lambda b, h, i: (b, blk(i), h))
    out = jax.ShapeDtypeStruct((B, T, H * dv), F32)
    return pl.pallas_call(
        _gla_kernel,
        grid=(B, H // hps, n_blk),
        in_specs=[xspec(blk_f), xspec(blk_b), lspec(blk_f), lspec(blk_b),
                  tspec(blk_f), tspec(blk_f), tspec(blk_b), tspec(blk_b),
                  pl.BlockSpec((hps, 2, 128, GLA_DK), lambda b, h, i: (h, 0, 0, 0)),
                  pl.BlockSpec((hps, 2, 1, GLA_DK), lambda b, h, i: (h, 0, 0, 0))],
        out_specs=[ospec(blk_f), ospec(blk_b)],
        out_shape=[out, out],
        scratch_shapes=[pltpu.VMEM((2, hps, GLA_DK, dv), F32)],
        compiler_params=_params("arbitrary", "arbitrary", "arbitrary"),
        name="gla_scan",
    )(gl, gl, lr, lr, cos_tab, sin_tab, cos_tab, sin_tab, up_tied, bias_tied)


def _gla_finish_kernel(of_ref, ob_ref, g_ref, gn_ref, y_ref, *, H, dv):
    gn = gn_ref[...]
    for h in range(H):
        sl = slice(h * dv, (h + 1) * dv)
        o = of_ref[0, :, sl] + ob_ref[0, :, sl]
        o = o * lax.rsqrt(jnp.mean(o * o, axis=-1, keepdims=True) + NORM_EPS) * gn
        g = g_ref[0, :, sl]
        y_ref[0, :, sl] = (o * (g * jax.nn.sigmoid(g))).astype(y_ref.dtype)


def gla_finish(o_f, o_b, g, gnorm):
    B, T, W = o_f.shape
    dv = gnorm.shape[0]
    H = W // dv
    spec = pl.BlockSpec((1, ROW_TILE, W), lambda b, i: (b, i, 0))
    return pl.pallas_call(
        functools.partial(_gla_finish_kernel, H=H, dv=dv),
        grid=(B, T // ROW_TILE),
        in_specs=[spec, spec, spec, pl.BlockSpec((1, dv), lambda b, i: (0, 0))],
        out_specs=spec,
        out_shape=jax.ShapeDtypeStruct((B, T, W), BF16),
        compiler_params=_params("arbitrary", "arbitrary"),
        name="gla_finish",
    )(o_f, o_b, g, gnorm.reshape(1, dv))


def rope_tables(S, T):
    half = GLA_DK // 2
    nf = half // 2
    pos = jnp.arange(S)
    inv_freq = ROPE_THETA ** (-jnp.arange(nf, dtype=F32) / nf)

    def part(p):
        ang = p.astype(F32)[:, None] * inv_freq[None, :]
        c, s = jnp.cos(ang), jnp.sin(ang)
        return jnp.concatenate([c, c], axis=-1), jnp.concatenate([-s, s], axis=-1)

    cr, sr = part(pos // GRID_W)
    cc, sc = part(pos % GRID_W)
    cos = jnp.concatenate([cr, cc], axis=-1)
    sin = jnp.concatenate([sr, sc], axis=-1)
    cos = jnp.concatenate([cos, jnp.ones((T - S, GLA_DK), F32)], axis=0)
    sin = jnp.concatenate([sin, jnp.zeros((T - S, GLA_DK), F32)], axis=0)
    return cos, sin


def _rot_partner_cols(w):
    K = w.shape[0]
    w4 = w.reshape(K, GLA_HEADS, 4, GLA_DK // 4)
    return w4[:, :, jnp.array([1, 0, 3, 2])].reshape(K, GLA_HEADS * GLA_DK)


def _tie_pairs(u):
    lead = u.shape[:-1]
    u = u.reshape(lead + (GLA_HEADS, 2, 1, GLA_DK // 4))
    u = jnp.broadcast_to(u, lead + (GLA_HEADS, 2, 2, GLA_DK // 4))
    return u.reshape(lead + (GLA_HEADS, GLA_DK))


def _moe_kernel(be_ref, bv_ref, x_ref, w1_ref, w3_ref, w2_ref, g_ref, o_ref, w1b, w3b, w2b):
    i = pl.program_id(0)
    changed = jnp.logical_or(i == 0, be_ref[i] != be_ref[jnp.maximum(i - 1, 0)])

    @pl.when(changed)
    def _():
        w1b[...] = w1_ref[0, 0].astype(BF16)
        w3b[...] = w3_ref[0, 0].astype(BF16)
        w2b[...] = w2_ref[0, 0].astype(BF16)

    @pl.when(bv_ref[i] != 0)
    def _():
        x = x_ref[...]
        h1 = jnp.dot(x, w1b[...], preferred_element_type=F32)
        h3 = jnp.dot(x, w3b[...], preferred_element_type=F32)
        a = (h1 * jax.nn.sigmoid(h1)) * h3
        y = jnp.dot(a.astype(BF16), w2b[...], preferred_element_type=F32)
        o_ref[...] = y * g_ref[...]

    @pl.when(bv_ref[i] == 0)
    def _():
        o_ref[...] = jnp.zeros_like(o_ref)


def moe_experts(xs, layer, w1, w3, w2, slot_gate, block_expert, block_valid):
    cap, D = xs.shape
    De = w1.shape[-1]
    n_blocks = cap // MOE_TILE
    return pl.pallas_call(
        _moe_kernel,
        grid_spec=pltpu.PrefetchScalarGridSpec(
            num_scalar_prefetch=2,
            grid=(n_blocks,),
            in_specs=[pl.BlockSpec((MOE_TILE, D), lambda i, be, bv: (i, 0)),
                      pl.BlockSpec((1, 1, D, De), lambda i, be, bv: (layer, be[i], 0, 0)),
                      pl.BlockSpec((1, 1, D, De), lambda i, be, bv: (layer, be[i], 0, 0)),
                      pl.BlockSpec((1, 1, De, D), lambda i, be, bv: (layer, be[i], 0, 0)),
                      pl.BlockSpec((MOE_TILE, 1), lambda i, be, bv: (i, 0))],
            out_specs=pl.BlockSpec((MOE_TILE, D), lambda i, be, bv: (i, 0)),
            scratch_shapes=[pltpu.VMEM((D, De), BF16), pltpu.VMEM((D, De), BF16), pltpu.VMEM((De, D), BF16)]),
        out_shape=jax.ShapeDtypeStruct((cap, D), F32),
        compiler_params=_params("arbitrary"),
        name="moe_experts",
    )(block_expert, block_valid, xs, w1, w3, w2, slot_gate)


def hier_moe(tok, w_router, b_router, layer, w1, w3, w2):
    N, D = tok.shape
    logits = mm(tok, w_router, F32, 1024 if N % 1024 == 0 else ROW_TILE, 128)[:, :N_GROUPS + N_EXPERTS] + b_router
    lg = logits[:, :N_GROUPS]
    grp = jnp.argmax(lg, axis=-1)
    p_grp = jnp.take_along_axis(jax.nn.softmax(lg, axis=-1), grp[:, None], axis=-1)
    le = logits[:, N_GROUPS:].reshape(N, N_GROUPS, EXPERTS_PER_GROUP)
    le = jnp.take_along_axis(le, grp[:, None, None], axis=1)[:, 0]
    top_v, top_i = lax.top_k(le, TOP_K_IN_GROUP)
    gate = p_grp * jax.nn.softmax(top_v, axis=-1)
    eid = (grp[:, None] * EXPERTS_PER_GROUP + top_i).astype(jnp.int32)
    A = N * TOP_K_IN_GROUP
    e_flat = eid.reshape(A)
    onehot = (e_flat[:, None] == jnp.arange(N_EXPERTS, dtype=jnp.int32)[None, :]).astype(jnp.int32)
    csum = jnp.cumsum(onehot, axis=0)
    rank = jnp.take_along_axis(csum, e_flat[:, None], axis=1)[:, 0] - 1
    counts = csum[-1]
    padded = (counts + MOE_TILE - 1) // MOE_TILE * MOE_TILE
    pend = jnp.cumsum(padded)
    pstart = pend - padded
    slot = pstart[e_flat] + rank
    n_blocks = -(-A // MOE_TILE) + N_EXPERTS
    cap = n_blocks * MOE_TILE
    slot_asg = jnp.full((cap,), -1, jnp.int32).at[slot].set(jnp.arange(A, dtype=jnp.int32))
    used = slot_asg >= 0
    slot_tok = jnp.where(used, slot_asg // TOP_K_IN_GROUP, N)
    slot_gate = jnp.where(used, gate.reshape(A)[jnp.maximum(slot_asg, 0)], 0.0)
    bstart = jnp.arange(n_blocks, dtype=jnp.int32) * MOE_TILE
    block_expert = jnp.minimum(jnp.searchsorted(pend, bstart, side='right'), N_EXPERTS - 1).astype(jnp.int32)
    block_valid = (bstart < pend[-1]).astype(jnp.int32)
    tok_pad = jnp.concatenate([tok, jnp.zeros((1, D), tok.dtype)], axis=0)
    xs = tok_pad[slot_tok]
    ys = moe_experts(xs, layer, w1, w3, w2, slot_gate[:, None], block_expert, block_valid)
    return ys[slot.reshape(N, TOP_K_IN_GROUP).T]


CONV_LANES = 128
HY_N2 = 128


def _conv3_time(x, w, S):
    T = x.shape[0]
    t = lax.broadcasted_iota(jnp.int32, x.shape, 0)
    prev = jnp.where((t == 0) | (t == S), 0.0, pltpu.roll(x, 1, 0))
    nxt = jnp.where((t == S - 1) | (t == T - 1), 0.0, pltpu.roll(x, T - 1, 0))
    return w[0:1] * prev + w[1:2] * x + w[2:3] * nxt


def _hy_pre_kernel(p_ref, w_ref, o_ref, *, S):
    o_ref[0, 0] = _conv3_time(p_ref[0], w_ref[...], S)


def hy_pre(p_hy, hy_short, S):
    B, T, C3 = p_hy.shape
    C = C3 // (HY_ORDER + 1)
    nj = C // CONV_LANES
    return pl.pallas_call(
        functools.partial(_hy_pre_kernel, S=S),
        grid=(B, HY_ORDER + 1, nj),
        in_specs=[pl.BlockSpec((1, T, CONV_LANES), lambda b, k, j: (b, 0, k * nj + j)),
                  pl.BlockSpec((CONV_W, CONV_LANES), lambda b, k, j: (0, k * nj + j))],
        out_specs=pl.BlockSpec((1, 1, T, CONV_LANES), lambda b, k, j: (k, b, 0, j)),
        out_shape=jax.ShapeDtypeStruct((HY_ORDER + 1, B, T, C), F32),
        compiler_params=_params("arbitrary", "arbitrary", "arbitrary"),
        name="hy_pre",
    )(p_hy, hy_short)


def _sc_mix_kernel(b_ref, c_ref, x_ref, w_ref, o_ref, *, S):
    o_ref[0] = (b_ref[0] * _conv3_time(c_ref[0] * x_ref[0], w_ref[...], S)).astype(o_ref.dtype)


def sc_mix(p_sc, sc_conv, S):
    B, T, C3 = p_sc.shape
    C = C3 // 3
    nj = C // CONV_LANES
    spec = lambda k: pl.BlockSpec((1, T, CONV_LANES), lambda b, j: (b, 0, k * nj + j))
    return pl.pallas_call(
        functools.partial(_sc_mix_kernel, S=S),
        grid=(B, nj),
        in_specs=[spec(0), spec(1), spec(2), pl.BlockSpec((CONV_W, CONV_LANES), lambda b, j: (0, j))],
        out_specs=pl.BlockSpec((1, T, CONV_LANES), lambda b, j: (b, 0, j)),
        out_shape=jax.ShapeDtypeStruct((B, T, C), BF16),
        compiler_params=_params("arbitrary", "arbitrary"),
        name="sc_mix",
    )(p_sc, p_sc, p_sc, sc_conv)


def _hyena_filters(L, C, w1, b1, w2, b2, w3, b3, w4):
    hp = lax.Precision.HIGHEST
    t = jnp.linspace(0.0, 1.0, L, dtype=F32)[:, None]
    bands = (HY_EMB - 1) // 2
    freqs = jnp.linspace(1e-4, bands - 1, bands, dtype=F32)[None, :]
    w = (2.0 * math.pi / L) * jnp.arange(L, dtype=F32)[:, None]
    emb = jnp.concatenate([t, jnp.cos(freqs * w), -jnp.sin(freqs * w)], axis=-1)
    a = jnp.sin(HY_SIN_FREQ * (jnp.dot(emb, w1, precision=hp) + b1))
    a = jnp.sin(HY_SIN_FREQ * (jnp.dot(a, w2, precision=hp) + b2))
    a = jnp.sin(HY_SIN_FREQ * (jnp.dot(a, w3, precision=hp) + b3))
    h = jnp.dot(a, w4, precision=hp).reshape(L, HY_ORDER, 2, C)
    deltas = jnp.abs(jnp.linspace(math.log(HY_DECAY_TARGET) / HY_SLOW_DECAY,
                                  math.log(HY_DECAY_TARGET) / HY_FAST_DECAY, C, dtype=F32))
    h = h * jnp.exp(-t * deltas)[:, None, None, :]
    return h / (jnp.sum(jnp.abs(h), axis=(0, 2), keepdims=True) + NORM_EPS)


def dft_tables(L):
    n = 2 * L
    N1 = n // HY_N2
    A = N1 // 2
    c = jnp.arange(N1, dtype=jnp.int32)
    a = jnp.arange(A, dtype=jnp.int32)
    ang1 = ((c[:, None] * a[None, :]) % N1).astype(F32) * (2.0 * math.pi / N1)
    fs = jnp.concatenate([jnp.cos(ang1), -jnp.sin(ang1)], axis=0)
    b = jnp.arange(HY_N2, dtype=jnp.int32)
    d = jnp.arange(HY_N2, dtype=jnp.int32)
    k2 = (b[None, None, :] * (c[:, None, None] + N1 * d[None, :, None])) % n
    ang2 = k2.astype(F32) * (2.0 * math.pi / n)
    cs, sn = jnp.cos(ang2), jnp.sin(ang2)
    gf = jnp.concatenate([jnp.concatenate([cs, sn], axis=2),
                          jnp.concatenate([-sn, cs], axis=2)], axis=1)
    return fs.astype(BF16), fs.T.astype(BF16), gf.astype(BF16), gf.transpose(0, 2, 1).astype(BF16)


DFT_SUB = 8


def _time_view(x):
    K, B, T, C = x.shape
    return x.reshape(K, B, T // HY_N2, HY_N2 // DFT_SUB, DFT_SUB, C)


def _dft_s1_kernel(f_ref, x_ref, o_ref):
    for bl in range(DFT_SUB):
        o_ref[0, bl] = _dot(f_ref[...], x_ref[:, 0, bl, :])


def dft_s1(fs, src, k, tc):
    _, B, _, C = src.shape
    M, A = fs.shape
    return pl.pallas_call(
        _dft_s1_kernel,
        grid=(B, C // tc, HY_N2 // DFT_SUB),
        in_specs=[pl.BlockSpec((M, A), lambda b, j, g: (0, 0)),
                  pl.BlockSpec((None, None, A, 1, DFT_SUB, tc), lambda b, j, g: (k, b, 0, g, 0, j))],
        out_specs=pl.BlockSpec((1, DFT_SUB, M, tc), lambda b, j, g: (b, g, 0, j)),
        out_shape=jax.ShapeDtypeStruct((B, HY_N2, M, C), F32),
        compiler_params=_params("arbitrary", "arbitrary", "arbitrary"),
        name="dft_s1",
    )(fs, _time_view(src))


def _gather_b(y_ref, cl):
    return jnp.concatenate([y_ref[0, :, 0, 0, cl, :], y_ref[0, :, 1, 0, cl, :]], axis=0)


def _dft_mid_kernel(gf_ref, gi_ref, y_ref, h_ref, o_ref):
    for cl in range(DFT_SUB):
        x = _dot(gf_ref[cl], _gather_b(y_ref, cl))
        xr, xi = x[:HY_N2], x[HY_N2:]
        hr, hi = h_ref[0, 0, 0, cl], h_ref[0, 1, 0, cl]
        p = jnp.concatenate([xr * hr - xi * hi, xr * hi + xi * hr], axis=0)
        o_ref[0, cl] = _dot(gi_ref[cl], p)


def dft_mid(gf, gi, y, hs, order, tc):
    B, N2, M, C = y.shape
    N1 = M // 2
    y6 = y.reshape(B, N2, 2, N1 // DFT_SUB, DFT_SUB, C)
    hs6 = hs.reshape(HY_ORDER, 2, N1 // DFT_SUB, DFT_SUB, N2, C)
    return pl.pallas_call(
        _dft_mid_kernel,
        grid=(N1 // DFT_SUB, C // tc, B),
        in_specs=[pl.BlockSpec((DFT_SUB, 2 * N2, 2 * N2), lambda c, j, b: (c, 0, 0)),
                  pl.BlockSpec((DFT_SUB, 2 * N2, 2 * N2), lambda c, j, b: (c, 0, 0)),
                  pl.BlockSpec((1, N2, 2, 1, DFT_SUB, tc), lambda c, j, b: (b, 0, 0, c, 0, j)),
                  pl.BlockSpec((1, 2, 1, DFT_SUB, N2, tc), lambda c, j, b: (order, 0, c, 0, 0, j))],
        out_specs=pl.BlockSpec((1, DFT_SUB, 2 * N2, tc), lambda c, j, b: (b, c, 0, j)),
        out_shape=jax.ShapeDtypeStruct((B, N1, 2 * N2, C), F32),
        compiler_params=_params("arbitrary", "arbitrary", "arbitrary"),
        name="dft_mid",
    )(gf, gi, y6, hs6)


def _dft_filt_kernel(gf_ref, yf_ref, yb_ref, o_ref):
    for cl in range(DFT_SUB):
        xf = _dot(gf_ref[cl], _gather_b(yf_ref, cl))
        xb = _dot(gf_ref[cl], _gather_b(yb_ref, cl))
        o_ref[0, 0, 0, cl] = xf[:HY_N2] + xb[:HY_N2]
        o_ref[0, 1, 0, cl] = xf[HY_N2:] - xb[HY_N2:]


def dft_filter_spectrum(gf, y, C, tc):
    _, N2, M, CC = y.shape
    N1 = M // 2
    nj = C // tc
    y6 = y.reshape(1, N2, 2, N1 // DFT_SUB, DFT_SUB, CC)
    out = pl.pallas_call(
        _dft_filt_kernel,
        grid=(N1 // DFT_SUB, HY_ORDER, nj),
        in_specs=[pl.BlockSpec((DFT_SUB, 2 * N2, 2 * N2), lambda c, o, j: (c, 0, 0)),
                  pl.BlockSpec((1, N2, 2, 1, DFT_SUB, tc), lambda c, o, j: (0, 0, 0, c, 0, (2 * o) * nj + j)),
                  pl.BlockSpec((1, N2, 2, 1, DFT_SUB, tc), lambda c, o, j: (0, 0, 0, c, 0, (2 * o + 1) * nj + j))],
        out_specs=pl.BlockSpec((1, 2, 1, DFT_SUB, N2, tc), lambda c, o, j: (o, 0, c, 0, 0, j)),
        out_shape=jax.ShapeDtypeStruct((HY_ORDER, 2, N1 // DFT_SUB, DFT_SUB, N2, C), F32),
        compiler_params=_params("arbitrary", "arbitrary", "arbitrary"),
        name="dft_filter_spectrum",
    )(gf, y6, y6)
    return out.reshape(HY_ORDER, 2, N1, N2, C)


def _dft_i2_kernel(f_ref, q_ref, z_ref, g_ref, bias_ref, o_ref, *, inv_n):
    for bl in range(DFT_SUB):
        y = _dot(f_ref[...], _gather_b(q_ref, bl))
        z = z_ref[:, 0, bl, :]
        o_ref[:, 0, bl, :] = g_ref[:, 0, bl, :] * (y * inv_n + bias_ref[...] * z)


def dft_i2(fi, q, zsrc, kz, gsrc, kg, bias, tc):
    A, M = fi.shape
    B, N1, _, C = q.shape
    ng = HY_N2 // DFT_SUB
    q6 = q.reshape(B, N1, 2, ng, DFT_SUB, C)
    tspec = lambda k: pl.BlockSpec((None, None, A, 1, DFT_SUB, tc), lambda b, j, g: (k, b, 0, g, 0, j))
    out = pl.pallas_call(
        functools.partial(_dft_i2_kernel, inv_n=1.0 / (N1 * HY_N2)),
        grid=(B, C // tc, ng),
        in_specs=[pl.BlockSpec((A, M), lambda b, j, g: (0, 0)),
                  pl.BlockSpec((1, N1, 2, 1, DFT_SUB, tc), lambda b, j, g: (b, 0, 0, g, 0, j)),
                  tspec(kz), tspec(kg),
                  pl.BlockSpec((1, tc), lambda b, j, g: (0, j))],
        out_specs=pl.BlockSpec((None, A, 1, DFT_SUB, tc), lambda b, j, g: (b, 0, g, 0, j)),
        out_shape=jax.ShapeDtypeStruct((B, A, ng, DFT_SUB, C), F32),
        compiler_params=_params("arbitrary", "arbitrary", "arbitrary"),
        name="dft_i2",
    )(fi, q6, _time_view(zsrc), _time_view(gsrc), bias)
    return out.reshape(B, A * HY_N2, C)


def _hy_ctx_kernel(fd_ref, fdi_ref, pc_ref, h_ref, bias_ref, o_ref, *, n):
    hp = lax.Precision.HIGHEST
    dot = lambda a, b: jnp.dot(a, b, preferred_element_type=F32, precision=hp)
    fd = fd_ref[...]
    z = pc_ref[0, 0]
    for o in range(HY_ORDER):
        hf = dot(fd, h_ref[2 * o])
        hb = dot(fd, h_ref[2 * o + 1])
        hr = hf[:n] + hb[:n]
        hi = hf[n:] - hb[n:]
        x = dot(fd, z)
        xr, xi = x[:n], x[n:]
        p = jnp.concatenate([xr * hr - xi * hi, xr * hi + xi * hr], axis=0)
        y = dot(fdi_ref[...], p) * (1.0 / n)
        z = pc_ref[o + 1, 0] * (y + bias_ref[o:o + 1] * z)
    o_ref[0] = z


def hy_ctx(pc, filt_ctx, hy_bias, S):
    _, B, T, C = pc.shape
    Lc = T - S
    n = 2 * Lc
    f = jnp.arange(n, dtype=jnp.int32)
    t = jnp.arange(Lc, dtype=jnp.int32)
    ang = ((f[:, None] * t[None, :]) % n).astype(F32) * (2.0 * math.pi / n)
    fd = jnp.concatenate([jnp.cos(ang), -jnp.sin(ang)], axis=0)
    h4 = filt_ctx.reshape(Lc, 2 * HY_ORDER, C).transpose(1, 0, 2)
    tc = 256
    return pl.pallas_call(
        functools.partial(_hy_ctx_kernel, n=n),
        grid=(B, C // tc),
        in_specs=[pl.BlockSpec((2 * n, Lc), lambda b, j: (0, 0)),
                  pl.BlockSpec((Lc, 2 * n), lambda b, j: (0, 0)),
                  pl.BlockSpec((HY_ORDER + 1, 1, Lc, tc), lambda b, j: (0, b, S // Lc, j)),
                  pl.BlockSpec((2 * HY_ORDER, Lc, tc), lambda b, j: (0, 0, j)),
                  pl.BlockSpec((HY_ORDER, tc), lambda b, j: (0, j))],
        out_specs=pl.BlockSpec((1, Lc, tc), lambda b, j: (b, 0, j)),
        out_shape=jax.ShapeDtypeStruct((B, Lc, C), F32),
        compiler_params=_params("arbitrary", "arbitrary"),
        name="hy_ctx",
    )(fd, fd.T, pc, h4, hy_bias)


def hyena_latent(pc, filt, hy_bias, S, tabs):
    fs, fi, gf, gi = tabs
    _, B, T, C = pc.shape
    hy = dft_s1(fs, filt.reshape(1, 1, S, 2 * HY_ORDER * C), 0, 512)
    hs = dft_filter_spectrum(gf, hy, C, 512)
    zsrc, kz = pc, 0
    for o in range(HY_ORDER):
        y = dft_s1(fs, zsrc, kz, 512)
        q = dft_mid(gf, gi, y, hs, o, 512)
        z = dft_i2(fi, q, zsrc, kz, pc, o + 1, hy_bias[o].reshape(1, C), 512)
        zsrc, kz = z.reshape(1, B, S, C), 0
    return z


def _flat(a):
    return a.reshape(a.shape[0] * a.shape[1], a.shape[2])


def _proj(h, w, out_dtype, tn):
    B, T, D = h.shape
    M = B * T
    tm = 1024 if M % 1024 == 0 else ROW_TILE
    return mm(_flat(h), w, out_dtype, tm, tn).reshape(B, T, w.shape[1])


def even_layer(h, xs, modtab, gain_ffn, S, w_in, w_out, rpb, gk_up, gk_bias, gnorm):
    B, T, D = xs.shape
    na_w = w_in.shape[1] - 2 * GLA_HEADS * GLA_DK - 2 * (D // 2) - 2 * GLA_LOWRANK
    na_w //= 3
    qk_w = GLA_HEADS * GLA_DK
    v_w = D // 2
    dv = v_w // GLA_HEADS
    o = 3 * na_w
    w_na = w_in[:, :o].astype(BF16)
    wq, wk = w_in[:, o:o + qk_w], w_in[:, o + qk_w:o + 2 * qk_w]
    wv = w_in[:, o + 2 * qk_w:o + 2 * qk_w + v_w]
    wg = w_in[:, o + 2 * qk_w + v_w:o + 2 * qk_w + 2 * v_w]
    wlr = w_in[:, o + 2 * qk_w + 2 * v_w:]
    per_head = lambda w, width: w.reshape(D, GLA_HEADS, width)
    w_gl = jnp.concatenate([per_head(wq, GLA_DK), per_head(_rot_partner_cols(wq), GLA_DK),
                            per_head(wk, GLA_DK), per_head(_rot_partner_cols(wk), GLA_DK),
                            per_head(wv, dv)], axis=-1).reshape(D, GLA_HEADS * GLA_HEAD_COLS).astype(BF16)
    w_g = wg.astype(BF16)
    w_lr = jnp.pad(wlr, ((0, 0), (0, 128 - 2 * GLA_LOWRANK))).astype(BF16)

    qkv = _proj(h, w_na, BF16, 1024)
    gl = _proj(h, w_gl, F32, 768)
    g = _proj(h, w_g, F32, 1024)
    lr = _proj(h, w_lr, F32, 128)

    ya = na_attention(qkv, na_bias_table(rpb, S), S)
    cos_tab, sin_tab = rope_tables(S, T)
    up_tied = _tie_pairs(gk_up.astype(F32)).transpose(2, 0, 1, 3)
    up_tied = jnp.stack([jnp.pad(up_tied[:, 0], ((0, 0), (0, 128 - GLA_LOWRANK), (0, 0))),
                         jnp.pad(up_tied[:, 1], ((0, 0), (GLA_LOWRANK, 128 - 2 * GLA_LOWRANK), (0, 0)))],
                        axis=1)
    bias_tied = _tie_pairs(gk_bias.astype(F32)).transpose(1, 0, 2)[:, :, None, :]
    o_f, o_b = gla_scan(gl, lr, cos_tab, sin_tab, up_tied, bias_tied, S)
    yb = gla_finish(o_f, o_b, g, gnorm)
    w_out_b = w_out.astype(BF16)
    return mm_res([ya, yb], [w_out_b[:na_w], w_out_b[na_w:]], xs, modtab, gain_ffn, S)


def odd_layer(h, xs, modtab, gain_ffn, S, ctx_live, w_in, w_out, hy_short, hyp, hy_bias, sc_conv, tabs):
    B, T, D = xs.shape
    C = D // 2
    n_hy = (HY_ORDER + 1) * C
    w = w_in.astype(BF16)
    pc = hy_pre(_proj(h, w[:, :n_hy], F32, 1024), hy_short, S)
    y_sc = sc_mix(_proj(h, w[:, n_hy:], F32, 1024), sc_conv, S)
    z_lat = hyena_latent(pc, _hyena_filters(S, C, *hyp), hy_bias, S, tabs)
    if ctx_live:
        z_ctx = hy_ctx(pc, _hyena_filters(T - S, C, *hyp), hy_bias, S)
    else:
        z_ctx = jnp.zeros((B, T - S, C), F32)
    z = jnp.concatenate([z_lat, z_ctx], axis=1)
    w_out_b = w_out.astype(BF16)
    return mm_res([z, y_sc], [w_out_b[:C], w_out_b[C:]], xs, modtab, gain_ffn, S)


def kernel(x, c, ctx, c_ctx, w_mod, b_mod, norm_mix, norm_ffn, norm_final, w_in_even, w_out_even, na_rpb,
           gla_gk_up, gla_gk_bias, gla_norm, w_in_odd, w_out_odd, hy_short, hy_w1, hy_b1, hy_w2, hy_b2,
           hy_w3, hy_b3, hy_w4, hy_bias, sc_conv, moe_w_group, moe_b_group, moe_w_expert, moe_b_expert,
           moe_w1, moe_w3, moe_w2):
    B, S, D = x.shape
    Lc = ctx.shape[1]
    T = S + Lc
    depth = w_mod.shape[0]
    last_even = 2 * ((depth - 1) // 2)
    xs = jnp.concatenate([x, ctx], axis=1)
    tabs = dft_tables(S)

    cvec = jnp.concatenate([c, c_ctx[None, :], jnp.zeros((8 - B - 1, D), F32)], axis=0)
    cvec = cvec * jax.nn.sigmoid(cvec)
    modtabs = []
    for l in range(depth):
        mod = mm(cvec, w_mod[l], F32, 8, 1024)[:B + 1] + b_mod[l]
        mod = mod.reshape(B + 1, 6, D)
        cx = jnp.broadcast_to(mod[B][None], (B, 6, D))
        modtabs.append(jnp.pad(jnp.stack([mod[:B], cx], axis=1), ((0, 0), (0, 0), (0, 2), (0, 0))))

    h = normmod(xs, norm_mix[0], modtabs[0], MOD_MIX, S, BF16)
    for l in range(depth):
        modtab = modtabs[l]
        if l % 2 == 0:
            e = l // 2
            xs, tok = even_layer(h, xs, modtab, norm_ffn[l], S, w_in_even[e], w_out_even[e], na_rpb[e],
                                 gla_gk_up[e], gla_gk_bias[e], gla_norm[e])
        else:
            o = l // 2
            hyp = (hy_w1[o], hy_b1[o], hy_w2[o], hy_b2[o], hy_w3[o], hy_b3[o], hy_w4[o])
            xs, tok = odd_layer(h, xs, modtab, norm_ffn[l], S, l < last_even, w_in_odd[o], w_out_odd[o],
                                hy_short[o], hyp, hy_bias[o], sc_conv[o], tabs)

        w_router = jnp.pad(jnp.concatenate([moe_w_group[l], moe_w_expert[l]], axis=1),
                           ((0, 0), (0, 128 - N_GROUPS - N_EXPERTS))).astype(BF16)
        b_router = jnp.concatenate([moe_b_group[l], moe_b_expert[l]])
        pair = hier_moe(_flat(tok), w_router, b_router, l, moe_w1, moe_w3, moe_w2)
        if l + 1 < depth:
            xs, h = moe_combine(xs, pair, modtab, norm_mix[l + 1], modtabs[l + 1], S, False)
        else:
            (out,) = moe_combine(xs, pair, modtab, norm_final, modtab, S, True)
    return out
```

```python
import functools
import math

import jax
import jax.numpy as jnp
import numpy as np
from jax import lax
from jax.experimental import pallas as pl
from jax.experimental.pallas import tpu as pltpu

F32 = jnp.float32
BF16 = jnp.bfloat16

NORM_EPS = 1e-6
NEG_INF = -1e30
ROPE_THETA = 10000.0
GRID_W = 64
HEAD_DIM = 128
NA_WIN_ROWS = 8
NA_WIN_COLS = 16
GLA_HEADS = 4
GLA_DK = 128
GLA_LOWRANK = 16
GLA_GATE_NORMALIZER = 16.0
GLA_CHUNK = 64
HY_ORDER = 2
HY_EMB = 33
HY_SIN_FREQ = 1.0
HY_DECAY_TARGET = 1e-2
HY_FAST_DECAY = 0.3
HY_SLOW_DECAY = 1.5
CONV_W = 3
N_GROUPS = 4
EXPERTS_PER_GROUP = 8
N_EXPERTS = N_GROUPS * EXPERTS_PER_GROUP
TOP_K_IN_GROUP = 2

ROW_TILE = 256
NA_GROUP_ROWS = 4
NA_SPAN_ROWS = NA_GROUP_ROWS + NA_WIN_ROWS
MOE_TILE = 512
VMEM_LIMIT = 56 * 1024 * 1024


def _params(*sem):
    return pltpu.CompilerParams(dimension_semantics=sem, vmem_limit_bytes=VMEM_LIMIT)


def _dot(a, b):
    return jnp.dot(a.astype(BF16), b.astype(BF16), preferred_element_type=F32)


def _dot_nt(a, b):
    return lax.dot_general(a.astype(BF16), b.astype(BF16), (((1,), (1,)), ((), ())),
                           preferred_element_type=F32)


def _mm_kernel(a_ref, w_ref, o_ref):
    o_ref[...] = _dot(a_ref[...], w_ref[...]).astype(o_ref.dtype)


def mm(a, w, out_dtype, tm, tn, layer=None):
    M, K = a.shape
    N = w.shape[-1]
    assert M % tm == 0 and N % tn == 0, (M, N, tm, tn)
    if layer is None:
        w_spec = pl.BlockSpec((K, tn), lambda j, i: (0, j))
    else:
        w_spec = pl.BlockSpec((None, K, tn), lambda j, i: (layer, 0, j))
    return pl.pallas_call(
        _mm_kernel,
        grid=(N // tn, M // tm),
        in_specs=[pl.BlockSpec((tm, K), lambda j, i: (i, 0)), w_spec],
        out_specs=pl.BlockSpec((tm, tn), lambda j, i: (i, j)),
        out_shape=jax.ShapeDtypeStruct((M, N), out_dtype),
        compiler_params=_params("arbitrary", "arbitrary"),
        name="mm",
    )(a, w)


MOD_MIX = (0, 1, 2)
MOD_FFN = (3, 4, 5)


def _norm_modulate(x, gain, mod, rows):
    y = x * lax.rsqrt(jnp.mean(x * x, axis=-1, keepdims=True) + NORM_EPS) * gain
    return y * (1.0 + mod[rows[1]:rows[1] + 1]) + mod[rows[0]:rows[0] + 1]


def _normmod_kernel(x_ref, g_ref, mod_ref, o_ref, *, rows):
    o_ref[0] = _norm_modulate(x_ref[0], g_ref[...], mod_ref[0, 0], rows).astype(o_ref.dtype)


def normmod(xs, gain, modtab, rows, S, out_dtype):
    B, T, D = xs.shape
    n_lat = S // ROW_TILE
    return pl.pallas_call(
        functools.partial(_normmod_kernel, rows=rows),
        grid=(B, T // ROW_TILE),
        in_specs=[pl.BlockSpec((1, ROW_TILE, D), lambda b, i: (b, i, 0)),
                  pl.BlockSpec((1, D), lambda b, i: (0, 0)),
                  pl.BlockSpec((1, 1, 8, D), lambda b, i: (b, jnp.where(i >= n_lat, 1, 0), 0, 0))],
        out_specs=pl.BlockSpec((1, ROW_TILE, D), lambda b, i: (b, i, 0)),
        out_shape=jax.ShapeDtypeStruct((B, T, D), out_dtype),
        compiler_params=_params("arbitrary", "arbitrary"),
        name="normmod",
    )(xs, gain.reshape(1, D), modtab)


def _mm_res_kernel(*refs, n_a):
    a_refs = refs[:n_a]
    w_refs = refs[n_a:2 * n_a]
    res_ref, mod_ref, gain_ref, o_ref, t_ref = refs[2 * n_a:]
    acc = _dot(a_refs[0][0], w_refs[0][...])
    for a_ref, w_ref in zip(a_refs[1:], w_refs[1:]):
        acc = acc + _dot(a_ref[0], w_ref[...])
    mod = mod_ref[0, 0]
    x = res_ref[0] + mod[MOD_MIX[2]:MOD_MIX[2] + 1] * acc
    o_ref[0] = x
    t_ref[0] = _norm_modulate(x, gain_ref[...], mod, MOD_FFN).astype(t_ref.dtype)


def mm_res(a_list, w_list, res, modtab, gain_ffn, S):
    B, T, D = res.shape
    n_lat = S // ROW_TILE
    n_a = len(a_list)
    row = pl.BlockSpec((1, ROW_TILE, D), lambda b, i: (b, i, 0))
    in_specs = [pl.BlockSpec((1, ROW_TILE, a.shape[-1]), lambda b, i: (b, i, 0)) for a in a_list]
    in_specs += [pl.BlockSpec(w.shape, lambda b, i: (0, 0)) for w in w_list]
    in_specs += [row, pl.BlockSpec((1, 1, 8, D), lambda b, i: (b, jnp.where(i >= n_lat, 1, 0), 0, 0)),
                 pl.BlockSpec((1, D), lambda b, i: (0, 0))]
    return pl.pallas_call(
        functools.partial(_mm_res_kernel, n_a=n_a),
        grid=(B, T // ROW_TILE),
        in_specs=in_specs,
        out_specs=[row, row],
        out_shape=[jax.ShapeDtypeStruct((B, T, D), F32), jax.ShapeDtypeStruct((B, T, D), BF16)],
        compiler_params=_params("arbitrary", "arbitrary"),
        name="mm_res",
    )(*a_list, *w_list, res, modtab, gain_ffn.reshape(1, D))


def _moe_combine_kernel(x_ref, p_ref, mod_ref, gain_ref, modn_ref, *out_refs, final):
    f = p_ref[0, 0].astype(F32) + p_ref[1, 0].astype(F32)
    x = x_ref[0] + mod_ref[0, 0][MOD_FFN[2]:MOD_FFN[2] + 1] * f
    if final:
        out_refs[0][0] = x * lax.rsqrt(jnp.mean(x * x, axis=-1, keepdims=True) + NORM_EPS) * gain_ref[...]
    else:
        out_refs[0][0] = x
        out_refs[1][0] = _norm_modulate(x, gain_ref[...], modn_ref[0, 0], MOD_MIX).astype(out_refs[1].dtype)


def moe_combine(xs, pair, modtab, gain_next, modtab_next, S, final):
    B, T, D = xs.shape
    n_lat = S // ROW_TILE
    rows = S if final else T
    row = pl.BlockSpec((1, ROW_TILE, D), lambda b, i: (b, i, 0))
    mspec = pl.BlockSpec((1, 1, 8, D), lambda b, i: (b, jnp.where(i >= n_lat, 1, 0), 0, 0))
    if final:
        out_specs, out_shape = [row], [jax.ShapeDtypeStruct((B, S, D), F32)]
    else:
        out_specs = [row, row]
        out_shape = [jax.ShapeDtypeStruct((B, T, D), F32), jax.ShapeDtypeStruct((B, T, D), BF16)]
    return pl.pallas_call(
        functools.partial(_moe_combine_kernel, final=final),
        grid=(B, rows // ROW_TILE),
        in_specs=[row, pl.BlockSpec((2, 1, ROW_TILE, D), lambda b, i: (0, b, i, 0)), mspec,
                  pl.BlockSpec((1, D), lambda b, i: (0, 0)), mspec],
        out_specs=out_specs,
        out_shape=out_shape,
        compiler_params=_params("arbitrary", "arbitrary"),
        name="moe_combine",
    )(xs, pair.reshape(2, B, T, D), modtab, gain_next.reshape(1, D), modtab_next)


NA_HEADS_PER_STEP = 4


def _na_kernel(q_ref, k_ref, v_ref, bias_ref, o_ref, *, S, Lc, n_lat):
    i = pl.program_id(2)
    rows = S // GRID_W
    scale = HEAD_DIM ** -0.5
    heads = [slice(h * HEAD_DIM, (h + 1) * HEAD_DIM) for h in range(NA_HEADS_PER_STEP)]
    q = [q_ref[0, :, hs] for hs in heads]
    vc = [v_ref[0, pl.ds(S, Lc), hs] for hs in heads]
    s_ctx = [_dot_nt(q[h], k_ref[0, pl.ds(S, Lc), hs]) * scale for h, hs in enumerate(heads)]

    @pl.when(i < n_lat)
    def _():
        ks = jnp.clip(NA_GROUP_ROWS * i - NA_WIN_ROWS // 2, 0, rows - NA_SPAN_ROWS)
        start = pl.multiple_of(ks * GRID_W, GRID_W)
        span = pl.ds(start, NA_SPAN_ROWS * GRID_W)
        s_loc = [_dot_nt(q[h], k_ref[0, span, hs]) * scale + bias_ref[h, 0] for h, hs in enumerate(heads)]
        m = [jnp.maximum(jnp.max(s_loc[h], axis=-1, keepdims=True), jnp.max(s_ctx[h], axis=-1, keepdims=True))
             for h in range(NA_HEADS_PER_STEP)]
        p_loc = [jnp.exp(s_loc[h] - m[h]) for h in range(NA_HEADS_PER_STEP)]
        p_ctx = [jnp.exp(s_ctx[h] - m[h]) for h in range(NA_HEADS_PER_STEP)]
        for h, hs in enumerate(heads):
            den = jnp.sum(p_loc[h], axis=-1, keepdims=True) + jnp.sum(p_ctx[h], axis=-1, keepdims=True)
            o = _dot(p_loc[h], v_ref[0, span, hs]) + _dot(p_ctx[h], vc[h])
            o_ref[0, :, hs] = (o / den).astype(o_ref.dtype)

    @pl.when(i >= n_lat)
    def _():
        for h, hs in enumerate(heads):
            m = jnp.max(s_ctx[h], axis=-1, keepdims=True)
            p = jnp.exp(s_ctx[h] - m)
            o = _dot(p, vc[h])
            o_ref[0, :, hs] = (o / jnp.sum(p, axis=-1, keepdims=True)).astype(o_ref.dtype)


def na_bias_table(rpb, S):
    rows = S // GRID_W
    n_lat = S // ROW_TILE
    kc = NA_WIN_COLS
    cols = jnp.arange(GRID_W)
    col_start = jnp.clip(cols - kc // 2, 0, GRID_W - kc)
    col_ok = (cols[None, :] >= col_start[:, None]) & (cols[None, :] < col_start[:, None] + kc)
    coff = jnp.clip(cols[None, :] - cols[:, None], -(kc - 1), kc - 1) + kc - 1
    cb = jnp.take(rpb.astype(F32), coff, axis=-1)
    cb = jnp.where(col_ok[None, None], cb, NEG_INF)
    g = jnp.array([0, min(1, n_lat - 1), n_lat - 1])
    ks = jnp.clip(NA_GROUP_ROWS * g - NA_WIN_ROWS // 2, 0, rows - NA_SPAN_ROWS)
    r = NA_GROUP_ROWS * g[:, None] + jnp.arange(NA_GROUP_ROWS)[None, :]
    k = ks[:, None] + jnp.arange(NA_SPAN_ROWS)[None, :]
    rs = jnp.clip(r - NA_WIN_ROWS // 2, 0, rows - NA_WIN_ROWS)
    valid = (k[:, None, :] >= rs[:, :, None]) & (k[:, None, :] < rs[:, :, None] + NA_WIN_ROWS)
    delta = jnp.clip(k[:, None, :] - r[:, :, None] + NA_WIN_ROWS - 1, 0, 2 * NA_WIN_ROWS - 2)
    t = cb[:, delta]
    t = jnp.where(valid[None, :, :, :, None, None], t, NEG_INF)
    t = t.transpose(0, 1, 2, 4, 3, 5)
    H = rpb.shape[0]
    return t.reshape(H, 3, NA_GROUP_ROWS * GRID_W, NA_SPAN_ROWS * GRID_W)


def na_attention(qkv, bias_tab, S):
    B, T, W3 = qkv.shape
    H = W3 // (3 * HEAD_DIM)
    Lc = T - S
    n_lat = S // ROW_TILE
    assert Lc == ROW_TILE and NA_GROUP_ROWS * GRID_W == ROW_TILE
    span = NA_SPAN_ROWS * GRID_W

    hps = NA_HEADS_PER_STEP
    hw = hps * HEAD_DIM
    ng = H // hps

    def bias_idx(b, h, i):
        return (h, jnp.where(i == 0, 0, jnp.where(i == n_lat - 1, 2, 1)), 0, 0)

    return pl.pallas_call(
        functools.partial(_na_kernel, S=S, Lc=Lc, n_lat=n_lat),
        grid=(B, ng, n_lat + 1),
        in_specs=[pl.BlockSpec((1, ROW_TILE, hw), lambda b, h, i: (b, i, h)),
                  pl.BlockSpec((1, T, hw), lambda b, h, i: (b, 0, ng + h)),
                  pl.BlockSpec((1, T, hw), lambda b, h, i: (b, 0, 2 * ng + h)),
                  pl.BlockSpec((hps, 1, ROW_TILE, span), bias_idx)],
        out_specs=pl.BlockSpec((1, ROW_TILE, hw), lambda b, h, i: (b, i, h)),
        out_shape=jax.ShapeDtypeStruct((B, T, H * HEAD_DIM), BF16),
        compiler_params=_params("arbitrary", "arbitrary", "arbitrary"),
        name="na_attention",
    )(qkv, qkv, qkv, bias_tab)


GLA_HEAD_COLS = 4 * GLA_DK + 2 * GLA_DK


GLA_HEADS_PER_STEP = 4


def _gla_prep(x, lr, cos, sin, up, gb, reverse):
    R = x.shape[0]
    C = GLA_CHUNK
    dk = GLA_DK
    q = (x[:, 0:dk] * cos + x[:, dk:2 * dk] * sin) * (dk ** -0.5)
    k = x[:, 2 * dk:3 * dk] * cos + x[:, 3 * dk:4 * dk] * sin
    v = x[:, 4 * dk:6 * dk]
    z = _dot(lr, up) + gb
    la = (jnp.minimum(z, 0.0) - jnp.log(1.0 + jnp.exp(-jnp.abs(z)))) / GLA_GATE_NORMALIZER
    ri = lax.broadcasted_iota(jnp.int32, (R, R), 0)
    ci = lax.broadcasted_iota(jnp.int32, (R, R), 1)
    same = (ri // C) == (ci // C)
    cum = jnp.logical_and(same, (ci >= ri) if reverse else (ci <= ri)).astype(BF16)
    la_hi = la.astype(BF16)
    la_lo = la - la_hi.astype(F32)
    b = _dot(cum, la_hi) + _dot(cum, la_lo)
    return q, k, v, b


def _gla_kernel(xf_ref, xb_ref, lrf_ref, lrb_ref, cf_ref, sf_ref, cb_ref, sb_ref, up_ref, gb_ref,
                of_ref, ob_ref, state_ref):
    @pl.when(pl.program_id(2) == 0)
    def _():
        state_ref[...] = jnp.zeros_like(state_ref)

    C = GLA_CHUNK
    dk = GLA_DK
    dv = 2 * dk
    n_chunks = ROW_TILE // C
    rc = lax.broadcasted_iota(jnp.int32, (C, C), 0)
    cc = lax.broadcasted_iota(jnp.int32, (C, C), 1)
    chains = []
    for h in range(GLA_HEADS_PER_STEP):
        cols = slice(h * GLA_HEAD_COLS, (h + 1) * GLA_HEAD_COLS)
        for d, (x_ref, lr_ref, c_ref, s_ref, o_ref) in enumerate(
                ((xf_ref, lrf_ref, cf_ref, sf_ref, of_ref), (xb_ref, lrb_ref, cb_ref, sb_ref, ob_ref))):
            q, k, v, b = _gla_prep(x_ref[0, :, cols], lr_ref[0], c_ref[...], s_ref[...], up_ref[h, d], gb_ref[h, d],
                                   d == 1)
            chains.append(dict(q=q, k=k, v=v, b=b, d=d, h=h, o_ref=o_ref, state=state_ref[d, h]))
    for step in range(n_chunks):
        for ch in chains:
            rev = ch["d"] == 1
            c = n_chunks - 1 - step if rev else step
            sl = slice(c * C, (c + 1) * C)
            mid = C // 2 if rev else C // 2 - 1
            last = 0 if rev else C - 1
            bc, qc, kc = ch["b"][sl], ch["q"][sl], ch["k"][sl]
            b_mid = bc[mid:mid + 1]
            b_last = bc[last:last + 1]
            att = _dot_nt(qc * jnp.exp(bc - b_mid), kc * jnp.exp(b_mid - bc))
            ch.update(sl=sl, att=jnp.where((cc >= rc) if rev else (cc <= rc), att, 0.0),
                      qb=qc * jnp.exp(bc), kd_t=(kc * jnp.exp(b_last - bc)).T,
                      dec=jnp.exp(jnp.broadcast_to(b_last, (dk, dk)).T[:, :1]))
        for ch in chains:
            vc = ch["v"][ch["sl"]]
            o = _dot(ch["att"], vc) + _dot(ch["qb"], ch["state"])
            ch["o_ref"][0, ch["sl"], ch["h"] * dv:(ch["h"] + 1) * dv] = o
            ch["state"] = ch["dec"] * ch["state"] + _dot(ch["kd_t"], vc)
    for ch in chains:
        state_ref[ch["d"], ch["h"]] = ch["state"]


def gla_scan(gl, lr, cos_tab, sin_tab, up_tied, bias_tied, S):
    B, T, W = gl.shape
    H = W // GLA_HEAD_COLS
    n_blk = T // ROW_TILE
    dv = 2 * GLA_DK
    hps = GLA_HEADS_PER_STEP
    blk_f = lambda i: jnp.where(i == 0, n_blk - 1, i - 1)
    blk_b = lambda i: jnp.where(i == 0, n_blk - 1, n_blk - 1 - i)
    xspec = lambda blk: pl.BlockSpec((1, ROW_TILE, hps * GLA_HEAD_COLS), lambda b, h, i: (b, blk(i), h))
    lspec = lambda blk: pl.BlockSpec((1, ROW_TILE, 128), lambda b, h, i: (b, blk(i), 0))
    tspec = lambda blk: pl.BlockSpec((ROW_TILE, GLA_DK), lambda b, h, i: (blk(i), 0))
    ospec = lambda blk: pl.BlockSpec((1, ROW_TILE, hps * dv), lambda b, h, i: (b, blk(i), h))
    out = jax.ShapeDtypeStruct((B, T, H * dv), F32)
    return pl.pallas_call(
        _gla_kernel,
        grid=(B, H // hps, n_blk),
        in_specs=[xspec(blk_f), xspec(blk_b), lspec(blk_f), lspec(blk_b),
                  tspec(blk_f), tspec(blk_f), tspec(blk_b), tspec(blk_b),
                  pl.BlockSpec((hps, 2, 128, GLA_DK), lambda b, h, i: (h, 0, 0, 0)),
                  pl.BlockSpec((hps, 2, 1, GLA_DK), lambda b, h, i: (h, 0, 0, 0))],
        out_specs=[ospec(blk_f), ospec(blk_b)],
        out_shape=[out, out],
        scratch_shapes=[pltpu.VMEM((2, hps, GLA_DK, dv), F32)],
        compiler_params=_params("arbitrary", "arbitrary", "arbitrary"),
        name="gla_scan",
    )(gl, gl, lr, lr, cos_tab, sin_tab, cos_tab, sin_tab, up_tied, bias_tied)


def _gla_finish_kernel(of_ref, ob_ref, g_ref, gn_ref, y_ref, *, H, dv):
    gn = gn_ref[...]
    for h in range(H):
        sl = slice(h * dv, (h + 1) * dv)
        o = of_ref[0, :, sl] + ob_ref[0, :, sl]
        o = o * lax.rsqrt(jnp.mean(o * o, axis=-1, keepdims=True) + NORM_EPS) * gn
        g = g_ref[0, :, sl]
        y_ref[0, :, sl] = (o * (g * jax.nn.sigmoid(g))).astype(y_ref.dtype)


def gla_finish(o_f, o_b, g, gnorm):
    B, T, W = o_f.shape
    dv = gnorm.shape[0]
    H = W // dv
    spec = pl.BlockSpec((1, ROW_TILE, W), lambda b, i: (b, i, 0))
    return pl.pallas_call(
        functools.partial(_gla_finish_kernel, H=H, dv=dv),
        grid=(B, T // ROW_TILE),
        in_specs=[spec, spec, spec, pl.BlockSpec((1, dv), lambda b, i: (0, 0))],
        out_specs=spec,
        out_shape=jax.ShapeDtypeStruct((B, T, W), BF16),
        compiler_params=_params("arbitrary", "arbitrary"),
        name="gla_finish",
    )(o_f, o_b, g, gnorm.reshape(1, dv))


def rope_tables(S, T):
    half = GLA_DK // 2
    nf = half // 2
    pos = jnp.arange(S)
    inv_freq = ROPE_THETA ** (-jnp.arange(nf, dtype=F32) / nf)

    def part(p):
        ang = p.astype(F32)[:, None] * inv_freq[None, :]
        c, s = jnp.cos(ang), jnp.sin(ang)
        return jnp.concatenate([c, c], axis=-1), jnp.concatenate([-s, s], axis=-1)

    cr, sr = part(pos // GRID_W)
    cc, sc = part(pos % GRID_W)
    cos = jnp.concatenate([cr, cc], axis=-1)
    sin = jnp.concatenate([sr, sc], axis=-1)
    cos = jnp.concatenate([cos, jnp.ones((T - S, GLA_DK), F32)], axis=0)
    sin = jnp.concatenate([sin, jnp.zeros((T - S, GLA_DK), F32)], axis=0)
    return cos, sin


def _rot_partner_cols(w):
    K = w.shape[0]
    w4 = w.reshape(K, GLA_HEADS, 4, GLA_DK // 4)
    return w4[:, :, jnp.array([1, 0, 3, 2])].reshape(K, GLA_HEADS * GLA_DK)


def _tie_pairs(u):
    lead = u.shape[:-1]
    u = u.reshape(lead + (GLA_HEADS, 2, 1, GLA_DK // 4))
    u = jnp.broadcast_to(u, lead + (GLA_HEADS, 2, 2, GLA_DK // 4))
    return u.reshape(lead + (GLA_HEADS, GLA_DK))


def _moe_kernel(be_ref, bv_ref, x_ref, w1_ref, w3_ref, w2_ref, g_ref, o_ref, w1b, w3b, w2b):
    i = pl.program_id(0)
    changed = jnp.logical_or(i == 0, be_ref[i] != be_ref[jnp.maximum(i - 1, 0)])

    @pl.when(changed)
    def _():
        w1b[...] = w1_ref[0, 0].astype(BF16)
        w3b[...] = w3_ref[0, 0].astype(BF16)
        w2b[...] = w2_ref[0, 0].astype(BF16)

    @pl.when(bv_ref[i] != 0)
    def _():
        x = x_ref[...]
        h1 = jnp.dot(x, w1b[...], preferred_element_type=F32)
        h3 = jnp.dot(x, w3b[...], preferred_element_type=F32)
        a = (h1 * jax.nn.sigmoid(h1)) * h3
        y = jnp.dot(a.astype(BF16), w2b[...], preferred_element_type=F32)
        g = g_ref[...]
        o_ref[...] = jnp.where(g != 0.0, y * g, 0.0).astype(o_ref.dtype)

    @pl.when(bv_ref[i] == 0)
    def _():
        o_ref[...] = jnp.zeros_like(o_ref)


def moe_experts(xs, layer, w1, w3, w2, slot_gate, block_expert, block_valid):
    cap, D = xs.shape
    De = w1.shape[-1]
    n_blocks = cap // MOE_TILE
    return pl.pallas_call(
        _moe_kernel,
        grid_spec=pltpu.PrefetchScalarGridSpec(
            num_scalar_prefetch=2,
            grid=(n_blocks,),
            in_specs=[pl.BlockSpec((MOE_TILE, D), lambda i, be, bv: (i, 0)),
                      pl.BlockSpec((1, 1, D, De), lambda i, be, bv: (layer, be[i], 0, 0)),
                      pl.BlockSpec((1, 1, D, De), lambda i, be, bv: (layer, be[i], 0, 0)),
                      pl.BlockSpec((1, 1, De, D), lambda i, be, bv: (layer, be[i], 0, 0)),
                      pl.BlockSpec((MOE_TILE, 1), lambda i, be, bv: (i, 0))],
            out_specs=pl.BlockSpec((MOE_TILE, D), lambda i, be, bv: (i, 0)),
            scratch_shapes=[pltpu.VMEM((D, De), BF16), pltpu.VMEM((D, De), BF16), pltpu.VMEM((De, D), BF16)]),
        out_shape=jax.ShapeDtypeStruct((cap, D), BF16),
        compiler_params=_params("arbitrary"),
        name="moe_experts",
    )(block_expert, block_valid, xs, w1, w3, w2, slot_gate)


def hier_moe(tok, w_router, b_router, layer, w1, w3, w2):
    N, D = tok.shape
    logits = mm(tok, w_router, F32, 1024 if N % 1024 == 0 else ROW_TILE, 128)[:, :N_GROUPS + N_EXPERTS] + b_router
    lg = logits[:, :N_GROUPS]
    grp = jnp.argmax(lg, axis=-1)
    p_grp = jnp.take_along_axis(jax.nn.softmax(lg, axis=-1), grp[:, None], axis=-1)
    le = logits[:, N_GROUPS:].reshape(N, N_GROUPS, EXPERTS_PER_GROUP)
    le = jnp.take_along_axis(le, grp[:, None, None], axis=1)[:, 0]
    top_v, top_i = lax.top_k(le, TOP_K_IN_GROUP)
    gate = p_grp * jax.nn.softmax(top_v, axis=-1)
    eid = (grp[:, None] * EXPERTS_PER_GROUP + top_i).astype(jnp.int32)
    A = N * TOP_K_IN_GROUP
    e_flat = eid.reshape(A)
    onehot = (e_flat[:, None] == jnp.arange(N_EXPERTS, dtype=jnp.int32)[None, :]).astype(jnp.int32)
    csum = jnp.cumsum(onehot, axis=0)
    rank = jnp.take_along_axis(csum, e_flat[:, None], axis=1)[:, 0] - 1
    counts = csum[-1]
    padded = (counts + MOE_TILE - 1) // MOE_TILE * MOE_TILE
    pend = jnp.cumsum(padded)
    pstart = pend - padded
    slot = pstart[e_flat] + rank
    n_blocks = -(-A // MOE_TILE) + N_EXPERTS
    cap = n_blocks * MOE_TILE
    slot_asg = jnp.full((cap,), -1, jnp.int32).at[slot].set(jnp.arange(A, dtype=jnp.int32))
    used = slot_asg >= 0
    slot_tok = jnp.where(used, slot_asg // TOP_K_IN_GROUP, jnp.arange(cap, dtype=jnp.int32) % N)
    slot_gate = jnp.where(used, gate.reshape(A)[jnp.maximum(slot_asg, 0)], 0.0)
    bstart = jnp.arange(n_blocks, dtype=jnp.int32) * MOE_TILE
    block_expert = jnp.minimum(jnp.sum((bstart[:, None] >= pend[None, :]).astype(jnp.int32), axis=1), N_EXPERTS - 1)
    block_valid = (bstart < pend[-1]).astype(jnp.int32)
    xs = tok[slot_tok]
    ys = moe_experts(xs, layer, w1, w3, w2, slot_gate[:, None], block_expert, block_valid)
    return ys[slot.reshape(N, TOP_K_IN_GROUP).T]


CONV_LANES = 128
HY_N2 = 128


def _conv3_time(x, w, S):
    T = x.shape[0]
    t = lax.broadcasted_iota(jnp.int32, x.shape, 0)
    prev = jnp.where((t == 0) | (t == S), 0.0, pltpu.roll(x, 1, 0))
    nxt = jnp.where((t == S - 1) | (t == T - 1), 0.0, pltpu.roll(x, T - 1, 0))
    return w[0:1] * prev + w[1:2] * x + w[2:3] * nxt


def _hy_pre_kernel(p_ref, w_ref, o_ref, *, S):
    o_ref[0, 0] = _conv3_time(p_ref[0], w_ref[...], S)


def hy_pre(p_hy, hy_short, S):
    B, T, C3 = p_hy.shape
    C = C3 // (HY_ORDER + 1)
    nj = C // CONV_LANES
    return pl.pallas_call(
        functools.partial(_hy_pre_kernel, S=S),
        grid=(B, HY_ORDER + 1, nj),
        in_specs=[pl.BlockSpec((1, T, CONV_LANES), lambda b, k, j: (b, 0, k * nj + j)),
                  pl.BlockSpec((CONV_W, CONV_LANES), lambda b, k, j: (0, k * nj + j))],
        out_specs=pl.BlockSpec((1, 1, T, CONV_LANES), lambda b, k, j: (k, b, 0, j)),
        out_shape=jax.ShapeDtypeStruct((HY_ORDER + 1, B, T, C), F32),
        compiler_params=_params("arbitrary", "arbitrary", "arbitrary"),
        name="hy_pre",
    )(p_hy, hy_short)


def _sc_mix_kernel(b_ref, c_ref, x_ref, w_ref, o_ref, *, S):
    o_ref[0] = (b_ref[0] * _conv3_time(c_ref[0] * x_ref[0], w_ref[...], S)).astype(o_ref.dtype)


def sc_mix(p_sc, sc_conv, S):
    B, T, C3 = p_sc.shape
    C = C3 // 3
    nj = C // CONV_LANES
    spec = lambda k: pl.BlockSpec((1, T, CONV_LANES), lambda b, j: (b, 0, k * nj + j))
    return pl.pallas_call(
        functools.partial(_sc_mix_kernel, S=S),
        grid=(B, nj),
        in_specs=[spec(0), spec(1), spec(2), pl.BlockSpec((CONV_W, CONV_LANES), lambda b, j: (0, j))],
        out_specs=pl.BlockSpec((1, T, CONV_LANES), lambda b, j: (b, 0, j)),
        out_shape=jax.ShapeDtypeStruct((B, T, C), BF16),
        compiler_params=_params("arbitrary", "arbitrary"),
        name="sc_mix",
    )(p_sc, p_sc, p_sc, sc_conv)


HY_EMB_PAD = 128


def _hy_filter_kernel(emb_ref, w1_ref, b1_ref, w2_ref, b2_ref, w3_ref, b3_ref, w4_ref, dl_ref, h_ref, s_ref):
    emb = emb_ref[...]
    a = jnp.sin(HY_SIN_FREQ * (_dot(emb, w1_ref[...]) + b1_ref[...]))
    a = jnp.sin(HY_SIN_FREQ * (_dot(a, w2_ref[...]) + b2_ref[...]))
    a = jnp.sin(HY_SIN_FREQ * (_dot(a, w3_ref[...]) + b3_ref[...]))
    t = emb[:, 0:1]
    h = _dot(a, w4_ref[...]) * jnp.exp(-t * dl_ref[...])
    h_ref[...] = h

    @pl.when(pl.program_id(1) == 0)
    def _():
        s_ref[...] = jnp.zeros_like(s_ref)

    s_ref[...] += jnp.sum(jnp.abs(h), axis=0, keepdims=True)


def hyena_filters(L, C, w1, b1, w2, b2, w3, b3, w4):
    t = jnp.linspace(0.0, 1.0, L, dtype=F32)[:, None]
    bands = (HY_EMB - 1) // 2
    freqs = jnp.linspace(1e-4, bands - 1, bands, dtype=F32)[None, :]
    w = (2.0 * math.pi / L) * jnp.arange(L, dtype=F32)[:, None]
    emb = jnp.concatenate([t, jnp.cos(freqs * w), -jnp.sin(freqs * w)], axis=-1)
    emb = jnp.pad(emb, ((0, 0), (0, HY_EMB_PAD - HY_EMB)))
    w1p = jnp.pad(w1, ((0, HY_EMB_PAD - HY_EMB), (0, 0)))
    F = w1.shape[1]
    CC = w4.shape[1]
    deltas = jnp.abs(jnp.linspace(math.log(HY_DECAY_TARGET) / HY_SLOW_DECAY,
                                  math.log(HY_DECAY_TARGET) / HY_FAST_DECAY, C, dtype=F32))
    dl = jnp.tile(deltas, CC // C).reshape(1, CC)
    tl = min(L, 1024)
    tn = 1024
    full = lambda shape: pl.BlockSpec(shape, lambda j, i: (0, 0))
    h, s = pl.pallas_call(
        _hy_filter_kernel,
        grid=(CC // tn, L // tl),
        in_specs=[pl.BlockSpec((tl, HY_EMB_PAD), lambda j, i: (i, 0)),
                  full((HY_EMB_PAD, F)), full((1, F)), full((F, F)), full((1, F)), full((F, F)), full((1, F)),
                  pl.BlockSpec((F, tn), lambda j, i: (0, j)), pl.BlockSpec((1, tn), lambda j, i: (0, j))],
        out_specs=[pl.BlockSpec((tl, tn), lambda j, i: (i, j)), pl.BlockSpec((1, tn), lambda j, i: (0, j))],
        out_shape=[jax.ShapeDtypeStruct((L, CC), F32), jax.ShapeDtypeStruct((1, CC), F32)],
        compiler_params=_params("arbitrary", "arbitrary"),
        name="hy_filter",
    )(emb, w1p, b1.reshape(1, F), w2, b2.reshape(1, F), w3, b3.reshape(1, F), w4, dl)
    tot = s.reshape(HY_ORDER, 2, C).sum(axis=1, keepdims=True) + NORM_EPS
    inv = (1.0 / jnp.broadcast_to(tot, (HY_ORDER, 2, C))).reshape(1, CC)
    return h, inv


def dft_tables(L):
    n = 2 * L
    N1 = n // HY_N2
    A = N1 // 2
    c = jnp.arange(N1, dtype=jnp.int32)
    a = jnp.arange(A, dtype=jnp.int32)
    ang1 = ((c[:, None] * a[None, :]) % N1).astype(F32) * (2.0 * math.pi / N1)
    fs = jnp.concatenate([jnp.cos(ang1), -jnp.sin(ang1)], axis=0)
    b = jnp.arange(HY_N2, dtype=jnp.int32)
    d = jnp.arange(HY_N2, dtype=jnp.int32)
    k2 = (b[None, None, :] * (c[:, None, None] + N1 * d[None, :, None])) % n
    ang2 = k2.astype(F32) * (2.0 * math.pi / n)
    cs, sn = jnp.cos(ang2), jnp.sin(ang2)
    gf = jnp.concatenate([jnp.concatenate([cs, sn], axis=2),
                          jnp.concatenate([-sn, cs], axis=2)], axis=1)
    return fs.astype(BF16), fs.T.astype(BF16), gf.astype(BF16), gf.transpose(0, 2, 1).astype(BF16)


DFT_SUB = 8


def _time_view(x):
    K, B, T, C = x.shape
    return x.reshape(K, B, T // HY_N2, HY_N2 // DFT_SUB, DFT_SUB, C)


def _dft_s1_kernel(f_ref, x_ref, o_ref):
    for bl in range(DFT_SUB):
        o_ref[0, bl] = _dot(f_ref[...], x_ref[:, 0, bl, :])


def dft_s1(fs, src, k, tc):
    _, B, _, C = src.shape
    M, A = fs.shape
    return pl.pallas_call(
        _dft_s1_kernel,
        grid=(B, C // tc, HY_N2 // DFT_SUB),
        in_specs=[pl.BlockSpec((M, A), lambda b, j, g: (0, 0)),
                  pl.BlockSpec((None, None, A, 1, DFT_SUB, tc), lambda b, j, g: (k, b, 0, g, 0, j))],
        out_specs=pl.BlockSpec((1, DFT_SUB, M, tc), lambda b, j, g: (b, g, 0, j)),
        out_shape=jax.ShapeDtypeStruct((B, HY_N2, M, C), F32),
        compiler_params=_params("arbitrary", "arbitrary", "arbitrary"),
        name="dft_s1",
    )(fs, _time_view(src))


def _gather_b(y_ref, cl):
    return jnp.concatenate([y_ref[0, :, 0, 0, cl, :], y_ref[0, :, 1, 0, cl, :]], axis=0)


def _dft_mid_kernel(gf_ref, gi_ref, y_ref, h_ref, o_ref):
    for cl in range(DFT_SUB):
        x = _dot(gf_ref[cl], _gather_b(y_ref, cl))
        xr, xi = x[:HY_N2], x[HY_N2:]
        hr, hi = h_ref[0, 0, 0, cl], h_ref[0, 1, 0, cl]
        p = jnp.concatenate([xr * hr - xi * hi, xr * hi + xi * hr], axis=0)
        o_ref[0, cl] = _dot(gi_ref[cl], p)


def dft_mid(gf, gi, y, hs, order, tc):
    B, N2, M, C = y.shape
    N1 = M // 2
    y6 = y.reshape(B, N2, 2, N1 // DFT_SUB, DFT_SUB, C)
    hs6 = hs.reshape(HY_ORDER, 2, N1 // DFT_SUB, DFT_SUB, N2, C)
    return pl.pallas_call(
        _dft_mid_kernel,
        grid=(N1 // DFT_SUB, C // tc, B),
        in_specs=[pl.BlockSpec((DFT_SUB, 2 * N2, 2 * N2), lambda c, j, b: (c, 0, 0)),
                  pl.BlockSpec((DFT_SUB, 2 * N2, 2 * N2), lambda c, j, b: (c, 0, 0)),
                  pl.BlockSpec((1, N2, 2, 1, DFT_SUB, tc), lambda c, j, b: (b, 0, 0, c, 0, j)),
                  pl.BlockSpec((1, 2, 1, DFT_SUB, N2, tc), lambda c, j, b: (order, 0, c, 0, 0, j))],
        out_specs=pl.BlockSpec((1, DFT_SUB, 2 * N2, tc), lambda c, j, b: (b, c, 0, j)),
        out_shape=jax.ShapeDtypeStruct((B, N1, 2 * N2, C), F32),
        compiler_params=_params("arbitrary", "arbitrary", "arbitrary"),
        name="dft_mid",
    )(gf, gi, y6, hs6)


def _dft_filt_kernel(gf_ref, yf_ref, yb_ref, inv_ref, o_ref):
    inv = inv_ref[...]
    for cl in range(DFT_SUB):
        xf = _dot(gf_ref[cl], _gather_b(yf_ref, cl))
        xb = _dot(gf_ref[cl], _gather_b(yb_ref, cl))
        o_ref[0, 0, 0, cl] = (xf[:HY_N2] + xb[:HY_N2]) * inv
        o_ref[0, 1, 0, cl] = (xf[HY_N2:] - xb[HY_N2:]) * inv


def dft_filter_spectrum(gf, y, inv, C, tc):
    _, N2, M, CC = y.shape
    N1 = M // 2
    nj = C // tc
    y6 = y.reshape(1, N2, 2, N1 // DFT_SUB, DFT_SUB, CC)
    out = pl.pallas_call(
        _dft_filt_kernel,
        grid=(N1 // DFT_SUB, HY_ORDER, nj),
        in_specs=[pl.BlockSpec((DFT_SUB, 2 * N2, 2 * N2), lambda c, o, j: (c, 0, 0)),
                  pl.BlockSpec((1, N2, 2, 1, DFT_SUB, tc), lambda c, o, j: (0, 0, 0, c, 0, (2 * o) * nj + j)),
                  pl.BlockSpec((1, N2, 2, 1, DFT_SUB, tc), lambda c, o, j: (0, 0, 0, c, 0, (2 * o + 1) * nj + j)),
                  pl.BlockSpec((1, tc), lambda c, o, j: (0, (2 * o) * nj + j))],
        out_specs=pl.BlockSpec((1, 2, 1, DFT_SUB, N2, tc), lambda c, o, j: (o, 0, c, 0, 0, j)),
        out_shape=jax.ShapeDtypeStruct((HY_ORDER, 2, N1 // DFT_SUB, DFT_SUB, N2, C), F32),
        compiler_params=_params("arbitrary", "arbitrary", "arbitrary"),
        name="dft_filter_spectrum",
    )(gf, y6, y6, inv)
    return out.reshape(HY_ORDER, 2, N1, N2, C)


def _dft_i2_kernel(f_ref, q_ref, z_ref, g_ref, bias_ref, o_ref, *, inv_n):
    for bl in range(DFT_SUB):
        y = _dot(f_ref[...], _gather_b(q_ref, bl))
        z = z_ref[:, 0, bl, :]
        o_ref[:, 0, bl, :] = g_ref[:, 0, bl, :] * (y * inv_n + bias_ref[...] * z)


def dft_i2(fi, q, zsrc, kz, gsrc, kg, bias, tc):
    A, M = fi.shape
    B, N1, _, C = q.shape
    ng = HY_N2 // DFT_SUB
    q6 = q.reshape(B, N1, 2, ng, DFT_SUB, C)
    tspec = lambda k: pl.BlockSpec((None, None, A, 1, DFT_SUB, tc), lambda b, j, g: (k, b, 0, g, 0, j))
    out = pl.pallas_call(
        functools.partial(_dft_i2_kernel, inv_n=1.0 / (N1 * HY_N2)),
        grid=(B, C // tc, ng),
        in_specs=[pl.BlockSpec((A, M), lambda b, j, g: (0, 0)),
                  pl.BlockSpec((1, N1, 2, 1, DFT_SUB, tc), lambda b, j, g: (b, 0, 0, g, 0, j)),
                  tspec(kz), tspec(kg),
                  pl.BlockSpec((1, tc), lambda b, j, g: (0, j))],
        out_specs=pl.BlockSpec((None, A, 1, DFT_SUB, tc), lambda b, j, g: (b, 0, g, 0, j)),
        out_shape=jax.ShapeDtypeStruct((B, A, ng, DFT_SUB, C), F32),
        compiler_params=_params("arbitrary", "arbitrary", "arbitrary"),
        name="dft_i2",
    )(fi, q6, _time_view(zsrc), _time_view(gsrc), bias)
    return out.reshape(B, A * HY_N2, C)


def _hy_ctx_kernel(fd_ref, fdi_ref, pc_ref, h_ref, bias_ref, o_ref, *, n):
    hp = lax.Precision.HIGHEST
    dot = lambda a, b: jnp.dot(a, b, preferred_element_type=F32, precision=hp)
    fd = fd_ref[...]
    z = pc_ref[0, 0]
    for o in range(HY_ORDER):
        hf = dot(fd, h_ref[2 * o])
        hb = dot(fd, h_ref[2 * o + 1])
        hr = hf[:n] + hb[:n]
        hi = hf[n:] - hb[n:]
        x = dot(fd, z)
        xr, xi = x[:n], x[n:]
        p = jnp.concatenate([xr * hr - xi * hi, xr * hi + xi * hr], axis=0)
        y = dot(fdi_ref[...], p) * (1.0 / n)
        z = pc_ref[o + 1, 0] * (y + bias_ref[o:o + 1] * z)
    o_ref[0] = z


def hy_ctx(pc, filt_ctx, hy_bias, S):
    _, B, T, C = pc.shape
    Lc = T - S
    n = 2 * Lc
    f = jnp.arange(n, dtype=jnp.int32)
    t = jnp.arange(Lc, dtype=jnp.int32)
    ang = ((f[:, None] * t[None, :]) % n).astype(F32) * (2.0 * math.pi / n)
    fd = jnp.concatenate([jnp.cos(ang), -jnp.sin(ang)], axis=0)
    h4 = filt_ctx.reshape(Lc, 2 * HY_ORDER, C).transpose(1, 0, 2)
    tc = 256
    return pl.pallas_call(
        functools.partial(_hy_ctx_kernel, n=n),
        grid=(B, C // tc),
        in_specs=[pl.BlockSpec((2 * n, Lc), lambda b, j: (0, 0)),
                  pl.BlockSpec((Lc, 2 * n), lambda b, j: (0, 0)),
                  pl.BlockSpec((HY_ORDER + 1, 1, Lc, tc), lambda b, j: (0, b, S // Lc, j)),
                  pl.BlockSpec((2 * HY_ORDER, Lc, tc), lambda b, j: (0, 0, j)),
                  pl.BlockSpec((HY_ORDER, tc), lambda b, j: (0, j))],
        out_specs=pl.BlockSpec((1, Lc, tc), lambda b, j: (b, 0, j)),
        out_shape=jax.ShapeDtypeStruct((B, Lc, C), F32),
        compiler_params=_params("arbitrary", "arbitrary"),
        name="hy_ctx",
    )(fd, fd.T, pc, h4, hy_bias)


def hyena_latent(pc, filt, hy_bias, S, tabs):
    fs, fi, gf, gi = tabs
    _, B, T, C = pc.shape
    h_raw, inv = filt
    hy = dft_s1(fs, h_raw.reshape(1, 1, S, 2 * HY_ORDER * C), 0, 512)
    hs = dft_filter_spectrum(gf, hy, inv, C, 512)
    zsrc, kz = pc, 0
    for o in range(HY_ORDER):
        y = dft_s1(fs, zsrc, kz, 512)
        q = dft_mid(gf, gi, y, hs, o, 512)
        z = dft_i2(fi, q, zsrc, kz, pc, o + 1, hy_bias[o].reshape(1, C), 512)
        zsrc, kz = z.reshape(1, B, S, C), 0
    return z


def _flat(a):
    return a.reshape(a.shape[0] * a.shape[1], a.shape[2])


def _proj(h, w, out_dtype, tn):
    B, T, D = h.shape
    M = B * T
    tm = 1024 if M % 1024 == 0 else ROW_TILE
    return mm(_flat(h), w, out_dtype, tm, tn).reshape(B, T, w.shape[1])


def even_layer(h, xs, modtab, gain_ffn, S, w_in, w_out, rpb, gk_up, gk_bias, gnorm):
    B, T, D = xs.shape
    na_w = w_in.shape[1] - 2 * GLA_HEADS * GLA_DK - 2 * (D // 2) - 2 * GLA_LOWRANK
    na_w //= 3
    qk_w = GLA_HEADS * GLA_DK
    v_w = D // 2
    dv = v_w // GLA_HEADS
    o = 3 * na_w
    w_na = w_in[:, :o].astype(BF16)
    wq, wk = w_in[:, o:o + qk_w], w_in[:, o + qk_w:o + 2 * qk_w]
    wv = w_in[:, o + 2 * qk_w:o + 2 * qk_w + v_w]
    wg = w_in[:, o + 2 * qk_w + v_w:o + 2 * qk_w + 2 * v_w]
    wlr = w_in[:, o + 2 * qk_w + 2 * v_w:]
    per_head = lambda w, width: w.reshape(D, GLA_HEADS, width)
    w_gl = jnp.concatenate([per_head(wq, GLA_DK), per_head(_rot_partner_cols(wq), GLA_DK),
                            per_head(wk, GLA_DK), per_head(_rot_partner_cols(wk), GLA_DK),
                            per_head(wv, dv)], axis=-1).reshape(D, GLA_HEADS * GLA_HEAD_COLS).astype(BF16)
    w_g = wg.astype(BF16)
    w_lr = jnp.pad(wlr, ((0, 0), (0, 128 - 2 * GLA_LOWRANK))).astype(BF16)

    qkv = _proj(h, w_na, BF16, 1024)
    gl = _proj(h, w_gl, F32, 768)
    g = _proj(h, w_g, F32, 1024)
    lr = _proj(h, w_lr, F32, 128)

    ya = na_attention(qkv, na_bias_table(rpb, S), S)
    cos_tab, sin_tab = rope_tables(S, T)
    up_tied = _tie_pairs(gk_up.astype(F32)).transpose(2, 0, 1, 3)
    up_tied = jnp.stack([jnp.pad(up_tied[:, 0], ((0, 0), (0, 128 - GLA_LOWRANK), (0, 0))),
                         jnp.pad(up_tied[:, 1], ((0, 0), (GLA_LOWRANK, 128 - 2 * GLA_LOWRANK), (0, 0)))],
                        axis=1)
    bias_tied = _tie_pairs(gk_bias.astype(F32)).transpose(1, 0, 2)[:, :, None, :]
    o_f, o_b = gla_scan(gl, lr, cos_tab, sin_tab, up_tied, bias_tied, S)
    yb = gla_finish(o_f, o_b, g, gnorm)
    w_out_b = w_out.astype(BF16)
    return mm_res([ya, yb], [w_out_b[:na_w], w_out_b[na_w:]], xs, modtab, gain_ffn, S)


def odd_layer(h, xs, modtab, gain_ffn, S, ctx_live, w_in, w_out, hy_short, hyp, hy_bias, sc_conv, tabs):
    B, T, D = xs.shape
    C = D // 2
    n_hy = (HY_ORDER + 1) * C
    w = w_in.astype(BF16)
    pc = hy_pre(_proj(h, w[:, :n_hy], F32, 1024), hy_short, S)
    y_sc = sc_mix(_proj(h, w[:, n_hy:], F32, 1024), sc_conv, S)
    z_lat = hyena_latent(pc, hyena_filters(S, C, *hyp), hy_bias, S, tabs)
    if ctx_live:
        h_ctx, inv_ctx = hyena_filters(T - S, C, *hyp)
        z_ctx = hy_ctx(pc, h_ctx * inv_ctx, hy_bias, S)
    else:
        z_ctx = jnp.zeros((B, T - S, C), F32)
    z = jnp.concatenate([z_lat, z_ctx], axis=1)
    w_out_b = w_out.astype(BF16)
    return mm_res([z, y_sc], [w_out_b[:C], w_out_b[C:]], xs, modtab, gain_ffn, S)


def kernel(x, c, ctx, c_ctx, w_mod, b_mod, norm_mix, norm_ffn, norm_final, w_in_even, w_out_even, na_rpb,
           gla_gk_up, gla_gk_bias, gla_norm, w_in_odd, w_out_odd, hy_short, hy_w1, hy_b1, hy_w2, hy_b2,
           hy_w3, hy_b3, hy_w4, hy_bias, sc_conv, moe_w_group, moe_b_group, moe_w_expert, moe_b_expert,
           moe_w1, moe_w3, moe_w2):
    B, S, D = x.shape
    Lc = ctx.shape[1]
    T = S + Lc
    depth = w_mod.shape[0]
    last_even = 2 * ((depth - 1) // 2)
    xs = jnp.concatenate([x, ctx], axis=1)
    tabs = dft_tables(S)

    cvec = jnp.concatenate([c, c_ctx[None, :], jnp.zeros((8 - B - 1, D), F32)], axis=0)
    cvec = cvec * jax.nn.sigmoid(cvec)
    modtabs = []
    for l in range(depth):
        mod = mm(cvec, w_mod, F32, 8, 1024, layer=l)[:B + 1] + b_mod[l]
        mod = mod.reshape(B + 1, 6, D)
        cx = jnp.broadcast_to(mod[B][None], (B, 6, D))
        modtabs.append(jnp.pad(jnp.stack([mod[:B], cx], axis=1), ((0, 0), (0, 0), (0, 2), (0, 0))))

    h = normmod(xs, norm_mix[0], modtabs[0], MOD_MIX, S, BF16)
    for l in range(depth):
        modtab = modtabs[l]
        if l % 2 == 0:
            e = l // 2
            xs, tok = even_layer(h, xs, modtab, norm_ffn[l], S, w_in_even[e], w_out_even[e], na_rpb[e],
                                 gla_gk_up[e], gla_gk_bias[e], gla_norm[e])
        else:
            o = l // 2
            hyp = (hy_w1[o], hy_b1[o], hy_w2[o], hy_b2[o], hy_w3[o], hy_b3[o], hy_w4[o])
            xs, tok = odd_layer(h, xs, modtab, norm_ffn[l], S, l < last_even, w_in_odd[o], w_out_odd[o],
                                hy_short[o], hyp, hy_bias[o], sc_conv[o], tabs)

        w_router = jnp.pad(jnp.concatenate([moe_w_group[l], moe_w_expert[l]], axis=1),
                           ((0, 0), (0, 128 - N_GROUPS - N_EXPERTS))).astype(BF16)
        b_router = jnp.concatenate([moe_b_group[l], moe_b_expert[l]])
        pair = hier_moe(_flat(tok), w_router, b_router, l, moe_w1, moe_w3, moe_w2)
        if l + 1 < depth:
            xs, h = moe_combine(xs, pair, modtab, norm_mix[l + 1], modtabs[l + 1], S, False)
        else:
            (out,) = moe_combine(xs, pair, modtab, norm_final, modtab, S, True)
    return out
```

```python
import functools
import math

import jax
import jax.numpy as jnp
import numpy as np
from jax import lax
from jax.experimental import pallas as pl
from jax.experimental.pallas import tpu as pltpu

F32 = jnp.float32
BF16 = jnp.bfloat16

NORM_EPS = 1e-6
NEG_INF = -1e30
ROPE_THETA = 10000.0
GRID_W = 64
HEAD_DIM = 128
NA_WIN_ROWS = 8
NA_WIN_COLS = 16
GLA_HEADS = 4
GLA_DK = 128
GLA_LOWRANK = 16
GLA_GATE_NORMALIZER = 16.0
GLA_CHUNK = 64
HY_ORDER = 2
HY_EMB = 33
HY_SIN_FREQ = 1.0
HY_DECAY_TARGET = 1e-2
HY_FAST_DECAY = 0.3
HY_SLOW_DECAY = 1.5
CONV_W = 3
N_GROUPS = 4
EXPERTS_PER_GROUP = 8
N_EXPERTS = N_GROUPS * EXPERTS_PER_GROUP
TOP_K_IN_GROUP = 2

ROW_TILE = 256
NA_GROUP_ROWS = 4
NA_SPAN_ROWS = NA_GROUP_ROWS + NA_WIN_ROWS
MOE_TILE = 512
VMEM_LIMIT = 56 * 1024 * 1024


def _params(*sem):
    return pltpu.CompilerParams(dimension_semantics=sem, vmem_limit_bytes=VMEM_LIMIT)


def _dot(a, b):
    return jnp.dot(a.astype(BF16), b.astype(BF16), preferred_element_type=F32)


def _dot_nt(a, b):
    return lax.dot_general(a.astype(BF16), b.astype(BF16), (((1,), (1,)), ((), ())),
                           preferred_element_type=F32)


def _mm_kernel(a_ref, w_ref, o_ref):
    o_ref[...] = _dot(a_ref[...], w_ref[...]).astype(o_ref.dtype)


def mm(a, w, out_dtype, tm, tn, layer=None):
    M, K = a.shape
    N = w.shape[-1]
    assert M % tm == 0 and N % tn == 0, (M, N, tm, tn)
    if layer is None:
        w_spec = pl.BlockSpec((K, tn), lambda j, i: (0, j))
    else:
        w_spec = pl.BlockSpec((None, K, tn), lambda j, i: (layer, 0, j))
    return pl.pallas_call(
        _mm_kernel,
        grid=(N // tn, M // tm),
        in_specs=[pl.BlockSpec((tm, K), lambda j, i: (i, 0)), w_spec],
        out_specs=pl.BlockSpec((tm, tn), lambda j, i: (i, j)),
        out_shape=jax.ShapeDtypeStruct((M, N), out_dtype),
        compiler_params=_params("arbitrary", "arbitrary"),
        name="mm",
    )(a, w)


MOD_MIX = (0, 1, 2)
MOD_FFN = (3, 4, 5)


def _norm_modulate(x, gain, mod, rows):
    y = x * lax.rsqrt(jnp.mean(x * x, axis=-1, keepdims=True) + NORM_EPS) * gain
    return y * (1.0 + mod[rows[1]:rows[1] + 1]) + mod[rows[0]:rows[0] + 1]


def _normmod_kernel(x_ref, g_ref, mod_ref, o_ref, *, rows):
    o_ref[0] = _norm_modulate(x_ref[0], g_ref[...], mod_ref[0, 0], rows).astype(o_ref.dtype)


def normmod(xs, gain, modtab, rows, S, out_dtype):
    B, T, D = xs.shape
    n_lat = S // ROW_TILE
    return pl.pallas_call(
        functools.partial(_normmod_kernel, rows=rows),
        grid=(B, T // ROW_TILE),
        in_specs=[pl.BlockSpec((1, ROW_TILE, D), lambda b, i: (b, i, 0)),
                  pl.BlockSpec((1, D), lambda b, i: (0, 0)),
                  pl.BlockSpec((1, 1, 8, D), lambda b, i: (b, jnp.where(i >= n_lat, 1, 0), 0, 0))],
        out_specs=pl.BlockSpec((1, ROW_TILE, D), lambda b, i: (b, i, 0)),
        out_shape=jax.ShapeDtypeStruct((B, T, D), out_dtype),
        compiler_params=_params("arbitrary", "arbitrary"),
        name="normmod",
    )(xs, gain.reshape(1, D), modtab)


def _mm_res_kernel(*refs, n_a):
    a_refs = refs[:n_a]
    w_refs = refs[n_a:2 * n_a]
    res_ref, mod_ref, gain_ref, o_ref, t_ref = refs[2 * n_a:]
    acc = _dot(a_refs[0][0], w_refs[0][...])
    for a_ref, w_ref in zip(a_refs[1:], w_refs[1:]):
        acc = acc + _dot(a_ref[0], w_ref[...])
    mod = mod_ref[0, 0]
    x = res_ref[0] + mod[MOD_MIX[2]:MOD_MIX[2] + 1] * acc
    o_ref[0] = x
    t_ref[0] = _norm_modulate(x, gain_ref[...], mod, MOD_FFN).astype(t_ref.dtype)


def mm_res(a_list, w_list, res, modtab, gain_ffn, S):
    B, T, D = res.shape
    n_lat = S // ROW_TILE
    n_a = len(a_list)
    row = pl.BlockSpec((1, ROW_TILE, D), lambda b, i: (b, i, 0))
    in_specs = [pl.BlockSpec((1, ROW_TILE, a.shape[-1]), lambda b, i: (b, i, 0)) for a in a_list]
    in_specs += [pl.BlockSpec(w.shape, lambda b, i: (0, 0)) for w in w_list]
    in_specs += [row, pl.BlockSpec((1, 1, 8, D), lambda b, i: (b, jnp.where(i >= n_lat, 1, 0), 0, 0)),
                 pl.BlockSpec((1, D), lambda b, i: (0, 0))]
    return pl.pallas_call(
        functools.partial(_mm_res_kernel, n_a=n_a),
        grid=(B, T // ROW_TILE),
        in_specs=in_specs,
        out_specs=[row, row],
        out_shape=[jax.ShapeDtypeStruct((B, T, D), F32), jax.ShapeDtypeStruct((B, T, D), BF16)],
        compiler_params=_params("arbitrary", "arbitrary"),
        name="mm_res",
    )(*a_list, *w_list, res, modtab, gain_ffn.reshape(1, D))


def _moe_combine_kernel(x_ref, p_ref, mod_ref, gain_ref, modn_ref, *out_refs, final):
    f = p_ref[0, 0].astype(F32) + p_ref[1, 0].astype(F32)
    x = x_ref[0] + mod_ref[0, 0][MOD_FFN[2]:MOD_FFN[2] + 1] * f
    if final:
        out_refs[0][0] = x * lax.rsqrt(jnp.mean(x * x, axis=-1, keepdims=True) + NORM_EPS) * gain_ref[...]
    else:
        out_refs[0][0] = x
        out_refs[1][0] = _norm_modulate(x, gain_ref[...], modn_ref[0, 0], MOD_MIX).astype(out_refs[1].dtype)


def moe_combine(xs, pair, modtab, gain_next, modtab_next, S, final):
    B, T, D = xs.shape
    n_lat = S // ROW_TILE
    rows = S if final else T
    row = pl.BlockSpec((1, ROW_TILE, D), lambda b, i: (b, i, 0))
    mspec = pl.BlockSpec((1, 1, 8, D), lambda b, i: (b, jnp.where(i >= n_lat, 1, 0), 0, 0))
    if final:
        out_specs, out_shape = [row], [jax.ShapeDtypeStruct((B, S, D), F32)]
    else:
        out_specs = [row, row]
        out_shape = [jax.ShapeDtypeStruct((B, T, D), F32), jax.ShapeDtypeStruct((B, T, D), BF16)]
    return pl.pallas_call(
        functools.partial(_moe_combine_kernel, final=final),
        grid=(B, rows // ROW_TILE),
        in_specs=[row, pl.BlockSpec((2, 1, ROW_TILE, D), lambda b, i: (0, b, i, 0)), mspec,
                  pl.BlockSpec((1, D), lambda b, i: (0, 0)), mspec],
        out_specs=out_specs,
        out_shape=out_shape,
        compiler_params=_params("arbitrary", "arbitrary"),
        name="moe_combine",
    )(xs, pair.reshape(2, B, T, D), modtab, gain_next.reshape(1, D), modtab_next)


NA_HEADS_PER_STEP = 4


def _na_kernel(q_ref, k_ref, v_ref, bias_ref, o_ref, *, S, Lc, n_lat):
    i = pl.program_id(2)
    rows = S // GRID_W
    scale = HEAD_DIM ** -0.5
    heads = [slice(h * HEAD_DIM, (h + 1) * HEAD_DIM) for h in range(NA_HEADS_PER_STEP)]
    q = [q_ref[0, :, hs] for hs in heads]
    vc = [v_ref[0, pl.ds(S, Lc), hs] for hs in heads]
    s_ctx = [_dot_nt(q[h], k_ref[0, pl.ds(S, Lc), hs]) * scale for h, hs in enumerate(heads)]

    @pl.when(i < n_lat)
    def _():
        ks = jnp.clip(NA_GROUP_ROWS * i - NA_WIN_ROWS // 2, 0, rows - NA_SPAN_ROWS)
        start = pl.multiple_of(ks * GRID_W, GRID_W)
        span = pl.ds(start, NA_SPAN_ROWS * GRID_W)
        s_loc = [_dot_nt(q[h], k_ref[0, span, hs]) * scale + bias_ref[h, 0] for h, hs in enumerate(heads)]
        m = [jnp.maximum(jnp.max(s_loc[h], axis=-1, keepdims=True), jnp.max(s_ctx[h], axis=-1, keepdims=True))
             for h in range(NA_HEADS_PER_STEP)]
        p_loc = [jnp.exp(s_loc[h] - m[h]) for h in range(NA_HEADS_PER_STEP)]
        p_ctx = [jnp.exp(s_ctx[h] - m[h]) for h in range(NA_HEADS_PER_STEP)]
        for h, hs in enumerate(heads):
            den = jnp.sum(p_loc[h], axis=-1, keepdims=True) + jnp.sum(p_ctx[h], axis=-1, keepdims=True)
            o = _dot(p_loc[h], v_ref[0, span, hs]) + _dot(p_ctx[h], vc[h])
            o_ref[0, :, hs] = (o / den).astype(o_ref.dtype)

    @pl.when(i >= n_lat)
    def _():
        for h, hs in enumerate(heads):
            m = jnp.max(s_ctx[h], axis=-1, keepdims=True)
            p = jnp.exp(s_ctx[h] - m)
            o = _dot(p, vc[h])
            o_ref[0, :, hs] = (o / jnp.sum(p, axis=-1, keepdims=True)).astype(o_ref.dtype)


def na_bias_table(rpb, S):
    rows = S // GRID_W
    n_lat = S // ROW_TILE
    kc = NA_WIN_COLS
    cols = jnp.arange(GRID_W)
    col_start = jnp.clip(cols - kc // 2, 0, GRID_W - kc)
    col_ok = (cols[None, :] >= col_start[:, None]) & (cols[None, :] < col_start[:, None] + kc)
    coff = jnp.clip(cols[None, :] - cols[:, None], -(kc - 1), kc - 1) + kc - 1
    cb = jnp.take(rpb.astype(F32), coff, axis=-1)
    cb = jnp.where(col_ok[None, None], cb, NEG_INF)
    g = jnp.array([0, min(1, n_lat - 1), n_lat - 1])
    ks = jnp.clip(NA_GROUP_ROWS * g - NA_WIN_ROWS // 2, 0, rows - NA_SPAN_ROWS)
    r = NA_GROUP_ROWS * g[:, None] + jnp.arange(NA_GROUP_ROWS)[None, :]
    k = ks[:, None] + jnp.arange(NA_SPAN_ROWS)[None, :]
    rs = jnp.clip(r - NA_WIN_ROWS // 2, 0, rows - NA_WIN_ROWS)
    valid = (k[:, None, :] >= rs[:, :, None]) & (k[:, None, :] < rs[:, :, None] + NA_WIN_ROWS)
    delta = jnp.clip(k[:, None, :] - r[:, :, None] + NA_WIN_ROWS - 1, 0, 2 * NA_WIN_ROWS - 2)
    t = cb[:, delta]
    t = jnp.where(valid[None, :, :, :, None, None], t, NEG_INF)
    t = t.transpose(0, 1, 2, 4, 3, 5)
    H = rpb.shape[0]
    return t.reshape(H, 3, NA_GROUP_ROWS * GRID_W, NA_SPAN_ROWS * GRID_W)


def na_attention(qkv, bias_tab, S):
    B, T, W3 = qkv.shape
    H = W3 // (3 * HEAD_DIM)
    Lc = T - S
    n_lat = S // ROW_TILE
    assert Lc == ROW_TILE and NA_GROUP_ROWS * GRID_W == ROW_TILE
    span = NA_SPAN_ROWS * GRID_W

    hps = NA_HEADS_PER_STEP
    hw = hps * HEAD_DIM
    ng = H // hps

    def bias_idx(b, h, i):
        return (h, jnp.where(i == 0, 0, jnp.where(i == n_lat - 1, 2, 1)), 0, 0)

    return pl.pallas_call(
        functools.partial(_na_kernel, S=S, Lc=Lc, n_lat=n_lat),
        grid=(B, ng, n_lat + 1),
        in_specs=[pl.BlockSpec((1, ROW_TILE, hw), lambda b, h, i: (b, i, h)),
                  pl.BlockSpec((1, T, hw), lambda b, h, i: (b, 0, ng + h)),
                  pl.BlockSpec((1, T, hw), lambda b, h, i: (b, 0, 2 * ng + h)),
                  pl.BlockSpec((hps, 1, ROW_TILE, span), bias_idx)],
        out_specs=pl.BlockSpec((1, ROW_TILE, hw), lambda b, h, i: (b, i, h)),
        out_shape=jax.ShapeDtypeStruct((B, T, H * HEAD_DIM), BF16),
        compiler_params=_params("arbitrary", "arbitrary", "arbitrary"),
        name="na_attention",
    )(qkv, qkv, qkv, bias_tab)


GLA_HEAD_COLS = 4 * GLA_DK + 2 * GLA_DK


GLA_HEADS_PER_STEP = 4


def _gla_prep(x, lr, cos, sin, up, gb, reverse):
    R = x.shape[0]
    C = GLA_CHUNK
    dk = GLA_DK
    q = (x[:, 0:dk] * cos + x[:, dk:2 * dk] * sin) * (dk ** -0.5)
    k = x[:, 2 * dk:3 * dk] * cos + x[:, 3 * dk:4 * dk] * sin
    v = x[:, 4 * dk:6 * dk]
    z = _dot(lr, up) + gb
    la = (jnp.minimum(z, 0.0) - jnp.log(1.0 + jnp.exp(-jnp.abs(z)))) / GLA_GATE_NORMALIZER
    ri = lax.broadcasted_iota(jnp.int32, (R, R), 0)
    ci = lax.broadcasted_iota(jnp.int32, (R, R), 1)
    same = (ri // C) == (ci // C)
    cum = jnp.logical_and(same, (ci >= ri) if reverse else (ci <= ri)).astype(BF16)
    la_hi = la.astype(BF16)
    la_lo = la - la_hi.astype(F32)
    b = _dot(cum, la_hi) + _dot(cum, la_lo)
    return q, k, v, b


def _gla_kernel(xf_ref, xb_ref, lrf_ref, lrb_ref, cf_ref, sf_ref, cb_ref, sb_ref, up_ref, gb_ref,
                of_ref, ob_ref, state_ref):
    @pl.when(pl.program_id(2) == 0)
    def _():
        state_ref[...] = jnp.zeros_like(state_ref)

    C = GLA_CHUNK
    dk = GLA_DK
    dv = 2 * dk
    n_chunks = ROW_TILE // C
    rc = lax.broadcasted_iota(jnp.int32, (C, C), 0)
    cc = lax.broadcasted_iota(jnp.int32, (C, C), 1)
    chains = []
    for h in range(GLA_HEADS_PER_STEP):
        cols = slice(h * GLA_HEAD_COLS, (h + 1) * GLA_HEAD_COLS)
        for d, (x_ref, lr_ref, c_ref, s_ref, o_ref) in enumerate(
                ((xf_ref, lrf_ref, cf_ref, sf_ref, of_ref), (xb_ref, lrb_ref, cb_ref, sb_ref, ob_ref))):
            q, k, v, b = _gla_prep(x_ref[0, :, cols], lr_ref[0], c_ref[...], s_ref[...], up_ref[h, d], gb_ref[h, d],
                                   d == 1)
            chains.append(dict(q=q, k=k, v=v, b=b, d=d, h=h, o_ref=o_ref, state=state_ref[d, h]))
    for step in range(n_chunks):
        for ch in chains:
            rev = ch["d"] == 1
            c = n_chunks - 1 - step if rev else step
            sl = slice(c * C, (c + 1) * C)
            mid = C // 2 if rev else C // 2 - 1
            last = 0 if rev else C - 1
            bc, qc, kc = ch["b"][sl], ch["q"][sl], ch["k"][sl]
            b_mid = bc[mid:mid + 1]
            b_last = bc[last:last + 1]
            att = _dot_nt(qc * jnp.exp(bc - b_mid), kc * jnp.exp(b_mid - bc))
            ch.update(sl=sl, att=jnp.where((cc >= rc) if rev else (cc <= rc), att, 0.0),
                      qb=qc * jnp.exp(bc), kd_t=(kc * jnp.exp(b_last - bc)).T,
                      dec=jnp.exp(jnp.broadcast_to(b_last, (dk, dk)).T[:, :1]))
        for ch in chains:
            vc = ch["v"][ch["sl"]]
            o = _dot(ch["att"], vc) + _dot(ch["qb"], ch["state"])
            ch["o_ref"][0, ch["sl"], ch["h"] * dv:(ch["h"] + 1) * dv] = o
            ch["state"] = ch["dec"] * ch["state"] + _dot(ch["kd_t"], vc)
    for ch in chains:
        state_ref[ch["d"], ch["h"]] = ch["state"]


def gla_scan(gl, lr, cos_tab, sin_tab, up_tied, bias_tied, S):
    B, T, W = gl.shape
    H = W // GLA_HEAD_COLS
    n_blk = T // ROW_TILE
    dv = 2 * GLA_DK
    hps = GLA_HEADS_PER_STEP
    blk_f = lambda i: jnp.where(i == 0, n_blk - 1, i - 1)
    blk_b = lambda i: jnp.where(i == 0, n_blk - 1, n_blk - 1 - i)
    xspec = lambda blk: pl.BlockSpec((1, ROW_TILE, hps * GLA_HEAD_COLS), lambda b, h, i: (b, blk(i), h))
    lspec = lambda blk: pl.BlockSpec((1, ROW_TILE, 128), lambda b, h, i: (b, blk(i), 0))
    tspec = lambda blk: pl.BlockSpec((ROW_TILE, GLA_DK), lambda b, h, i: (blk(i), 0))
    ospec = lambda blk: pl.BlockSpec((1, ROW_TILE, hps * dv), lambda b, h, i: (b, blk(i), h))
    out = jax.ShapeDtypeStruct((B, T, H * dv), F32)
    return pl.pallas_call(
        _gla_kernel,
        grid=(B, H // hps, n_blk),
        in_specs=[xspec(blk_f), xspec(blk_b), lspec(blk_f), lspec(blk_b),
                  tspec(blk_f), tspec(blk_f), tspec(blk_b), tspec(blk_b),
                  pl.BlockSpec((hps, 2, 128, GLA_DK), lambda b, h, i: (h, 0, 0, 0)),
                  pl.BlockSpec((hps, 2, 1, GLA_DK), lambda b, h, i: (h, 0, 0, 0))],
        out_specs=[ospec(blk_f), ospec(blk_b)],
        out_shape=[out, out],
        scratch_shapes=[pltpu.VMEM((2, hps, GLA_DK, dv), F32)],
        compiler_params=_params("arbitrary", "arbitrary", "arbitrary"),
        name="gla_scan",
    )(gl, gl, lr, lr, cos_tab, sin_tab, cos_tab, sin_tab, up_tied, bias_tied)


def _gla_finish_kernel(of_ref, ob_ref, g_ref, gn_ref, y_ref, *, H, dv):
    gn = gn_ref[...]
    for h in range(H):
        sl = slice(h * dv, (h + 1) * dv)
        o = of_ref[0, :, sl] + ob_ref[0, :, sl]
        o = o * lax.rsqrt(jnp.mean(o * o, axis=-1, keepdims=True) + NORM_EPS) * gn
        g = g_ref[0, :, sl]
        y_ref[0, :, sl] = (o * (g * jax.nn.sigmoid(g))).astype(y_ref.dtype)


def gla_finish(o_f, o_b, g, gnorm):
    B, T, W = o_f.shape
    dv = gnorm.shape[0]
    H = W // dv
    spec = pl.BlockSpec((1, ROW_TILE, W), lambda b, i: (b, i, 0))
    return pl.pallas_call(
        functools.partial(_gla_finish_kernel, H=H, dv=dv),
        grid=(B, T // ROW_TILE),
        in_specs=[spec, spec, spec, pl.BlockSpec((1, dv), lambda b, i: (0, 0))],
        out_specs=spec,
        out_shape=jax.ShapeDtypeStruct((B, T, W), BF16),
        compiler_params=_params("arbitrary", "arbitrary"),
        name="gla_finish",
    )(o_f, o_b, g, gnorm.reshape(1, dv))


def rope_tables(S, T):
    half = GLA_DK // 2
    nf = half // 2
    pos = jnp.arange(S)
    inv_freq = ROPE_THETA ** (-jnp.arange(nf, dtype=F32) / nf)

    def part(p):
        ang = p.astype(F32)[:, None] * inv_freq[None, :]
        c, s = jnp.cos(ang), jnp.sin(ang)
        return jnp.concatenate([c, c], axis=-1), jnp.concatenate([-s, s], axis=-1)

    cr, sr = part(pos // GRID_W)
    cc, sc = part(pos % GRID_W)
    cos = jnp.concatenate([cr, cc], axis=-1)
    sin = jnp.concatenate([sr, sc], axis=-1)
    cos = jnp.concatenate([cos, jnp.ones((T - S, GLA_DK), F32)], axis=0)
    sin = jnp.concatenate([sin, jnp.zeros((T - S, GLA_DK), F32)], axis=0)
    return cos, sin


def _rot_partner_cols(w):
    K = w.shape[0]
    w4 = w.reshape(K, GLA_HEADS, 4, GLA_DK // 4)
    return w4[:, :, jnp.array([1, 0, 3, 2])].reshape(K, GLA_HEADS * GLA_DK)


def _tie_pairs(u):
    lead = u.shape[:-1]
    u = u.reshape(lead + (GLA_HEADS, 2, 1, GLA_DK // 4))
    u = jnp.broadcast_to(u, lead + (GLA_HEADS, 2, 2, GLA_DK // 4))
    return u.reshape(lead + (GLA_HEADS, GLA_DK))


def _moe_kernel(be_ref, bv_ref, x_ref, w1_ref, w3_ref, w2_ref, g_ref, o_ref, w1b, w3b, w2b):
    i = pl.program_id(0)
    changed = jnp.logical_or(i == 0, be_ref[i] != be_ref[jnp.maximum(i - 1, 0)])

    @pl.when(changed)
    def _():
        w1b[...] = w1_ref[0, 0].astype(BF16)
        w3b[...] = w3_ref[0, 0].astype(BF16)
        w2b[...] = w2_ref[0, 0].astype(BF16)

    @pl.when(bv_ref[i] != 0)
    def _():
        x = x_ref[...]
        h1 = jnp.dot(x, w1b[...], preferred_element_type=F32)
        h3 = jnp.dot(x, w3b[...], preferred_element_type=F32)
        a = (h1 * jax.nn.sigmoid(h1)) * h3
        y = jnp.dot(a.astype(BF16), w2b[...], preferred_element_type=F32)
        g = g_ref[...]
        o_ref[...] = jnp.where(g != 0.0, y * g, 0.0).astype(o_ref.dtype)

    @pl.when(bv_ref[i] == 0)
    def _():
        o_ref[...] = jnp.zeros_like(o_ref)


def moe_experts(xs, layer, w1, w3, w2, slot_gate, block_expert, block_valid):
    cap, D = xs.shape
    De = w1.shape[-1]
    n_blocks = cap // MOE_TILE
    return pl.pallas_call(
        _moe_kernel,
        grid_spec=pltpu.PrefetchScalarGridSpec(
            num_scalar_prefetch=2,
            grid=(n_blocks,),
            in_specs=[pl.BlockSpec((MOE_TILE, D), lambda i, be, bv: (i, 0)),
                      pl.BlockSpec((1, 1, D, De), lambda i, be, bv: (layer, be[i], 0, 0)),
                      pl.BlockSpec((1, 1, D, De), lambda i, be, bv: (layer, be[i], 0, 0)),
                      pl.BlockSpec((1, 1, De, D), lambda i, be, bv: (layer, be[i], 0, 0)),
                      pl.BlockSpec((MOE_TILE, 1), lambda i, be, bv: (i, 0))],
            out_specs=pl.BlockSpec((MOE_TILE, D), lambda i, be, bv: (i, 0)),
            scratch_shapes=[pltpu.VMEM((D, De), BF16), pltpu.VMEM((D, De), BF16), pltpu.VMEM((De, D), BF16)]),
        out_shape=jax.ShapeDtypeStruct((cap, D), BF16),
        compiler_params=_params("arbitrary"),
        name="moe_experts",
    )(block_expert, block_valid, xs, w1, w3, w2, slot_gate)


def hier_moe(tok, w_router, b_router, layer, w1, w3, w2):
    N, D = tok.shape
    logits = mm(tok, w_router, F32, 1024 if N % 1024 == 0 else ROW_TILE, 128)[:, :N_GROUPS + N_EXPERTS] + b_router
    lg = logits[:, :N_GROUPS]
    grp = jnp.argmax(lg, axis=-1)
    p_grp = jnp.take_along_axis(jax.nn.softmax(lg, axis=-1), grp[:, None], axis=-1)
    le = logits[:, N_GROUPS:].reshape(N, N_GROUPS, EXPERTS_PER_GROUP)
    le = jnp.take_along_axis(le, grp[:, None, None], axis=1)[:, 0]
    top_v, top_i = lax.top_k(le, TOP_K_IN_GROUP)
    gate = p_grp * jax.nn.softmax(top_v, axis=-1)
    eid = (grp[:, None] * EXPERTS_PER_GROUP + top_i).astype(jnp.int32)
    A = N * TOP_K_IN_GROUP
    e_flat = eid.reshape(A)
    onehot = (e_flat[:, None] == jnp.arange(N_EXPERTS, dtype=jnp.int32)[None, :]).astype(jnp.int32)
    csum = jnp.cumsum(onehot, axis=0)
    rank = jnp.take_along_axis(csum, e_flat[:, None], axis=1)[:, 0] - 1
    counts = csum[-1]
    padded = (counts + MOE_TILE - 1) // MOE_TILE * MOE_TILE
    pend = jnp.cumsum(padded)
    pstart = pend - padded
    slot = pstart[e_flat] + rank
    n_blocks = -(-A // MOE_TILE) + N_EXPERTS
    cap = n_blocks * MOE_TILE
    slot_asg = jnp.full((cap,), -1, jnp.int32).at[slot].set(jnp.arange(A, dtype=jnp.int32))
    used = slot_asg >= 0
    slot_tok = jnp.where(used, slot_asg // TOP_K_IN_GROUP, jnp.arange(cap, dtype=jnp.int32) % N)
    slot_gate = jnp.where(used, gate.reshape(A)[jnp.maximum(slot_asg, 0)], 0.0)
    bstart = jnp.arange(n_blocks, dtype=jnp.int32) * MOE_TILE
    block_expert = jnp.minimum(jnp.sum((bstart[:, None] >= pend[None, :]).astype(jnp.int32), axis=1), N_EXPERTS - 1)
    block_valid = (bstart < pend[-1]).astype(jnp.int32)
    xs = tok[slot_tok]
    ys = moe_experts(xs, layer, w1, w3, w2, slot_gate[:, None], block_expert, block_valid)
    return ys[slot.reshape(N, TOP_K_IN_GROUP).T]


CONV_LANES = 128
HY_N2 = 128


def _conv3_time(x, w, S):
    T = x.shape[0]
    t = lax.broadcasted_iota(jnp.int32, x.shape, 0)
    prev = jnp.where((t == 0) | (t == S), 0.0, pltpu.roll(x, 1, 0))
    nxt = jnp.where((t == S - 1) | (t == T - 1), 0.0, pltpu.roll(x, T - 1, 0))
    return w[0:1] * prev + w[1:2] * x + w[2:3] * nxt


def _hy_pre_kernel(p_ref, w_ref, o_ref, *, S):
    o_ref[0, 0] = _conv3_time(p_ref[0], w_ref[...], S)


def hy_pre(p_hy, hy_short, S):
    B, T, C3 = p_hy.shape
    C = C3 // (HY_ORDER + 1)
    nj = C // CONV_LANES
    return pl.pallas_call(
        functools.partial(_hy_pre_kernel, S=S),
        grid=(B, HY_ORDER + 1, nj),
        in_specs=[pl.BlockSpec((1, T, CONV_LANES), lambda b, k, j: (b, 0, k * nj + j)),
                  pl.BlockSpec((CONV_W, CONV_LANES), lambda b, k, j: (0, k * nj + j))],
        out_specs=pl.BlockSpec((1, 1, T, CONV_LANES), lambda b, k, j: (k, b, 0, j)),
        out_shape=jax.ShapeDtypeStruct((HY_ORDER + 1, B, T, C), F32),
        compiler_params=_params("arbitrary", "arbitrary", "arbitrary"),
        name="hy_pre",
    )(p_hy, hy_short)


def _sc_mix_kernel(b_ref, c_ref, x_ref, w_ref, o_ref, *, S):
    o_ref[0] = (b_ref[0] * _conv3_time(c_ref[0] * x_ref[0], w_ref[...], S)).astype(o_ref.dtype)


def sc_mix(p_sc, sc_conv, S):
    B, T, C3 = p_sc.shape
    C = C3 // 3
    nj = C // CONV_LANES
    spec = lambda k: pl.BlockSpec((1, T, CONV_LANES), lambda b, j: (b, 0, k * nj + j))
    return pl.pallas_call(
        functools.partial(_sc_mix_kernel, S=S),
        grid=(B, nj),
        in_specs=[spec(0), spec(1), spec(2), pl.BlockSpec((CONV_W, CONV_LANES), lambda b, j: (0, j))],
        out_specs=pl.BlockSpec((1, T, CONV_LANES), lambda b, j: (b, 0, j)),
        out_shape=jax.ShapeDtypeStruct((B, T, C), BF16),
        compiler_params=_params("arbitrary", "arbitrary"),
        name="sc_mix",
    )(p_sc, p_sc, p_sc, sc_conv)


HY_EMB_PAD = 128


def _hy_filter_kernel(emb_ref, w1_ref, b1_ref, w2_ref, b2_ref, w3_ref, b3_ref, w4_ref, dl_ref, h_ref, s_ref):
    emb = emb_ref[...]
    a = jnp.sin(HY_SIN_FREQ * (_dot(emb, w1_ref[...]) + b1_ref[...]))
    a = jnp.sin(HY_SIN_FREQ * (_dot(a, w2_ref[...]) + b2_ref[...]))
    a = jnp.sin(HY_SIN_FREQ * (_dot(a, w3_ref[...]) + b3_ref[...]))
    t = emb[:, 0:1]
    h = _dot(a, w4_ref[...]) * jnp.exp(-t * dl_ref[...])
    h_ref[...] = h

    @pl.when(pl.program_id(1) == 0)
    def _():
        s_ref[...] = jnp.zeros_like(s_ref)

    s_ref[...] += jnp.sum(jnp.abs(h), axis=0, keepdims=True)


def hyena_filters(L, C, w1, b1, w2, b2, w3, b3, w4):
    t = jnp.linspace(0.0, 1.0, L, dtype=F32)[:, None]
    bands = (HY_EMB - 1) // 2
    freqs = jnp.linspace(1e-4, bands - 1, bands, dtype=F32)[None, :]
    w = (2.0 * math.pi / L) * jnp.arange(L, dtype=F32)[:, None]
    emb = jnp.concatenate([t, jnp.cos(freqs * w), -jnp.sin(freqs * w)], axis=-1)
    emb = jnp.pad(emb, ((0, 0), (0, HY_EMB_PAD - HY_EMB)))
    w1p = jnp.pad(w1, ((0, HY_EMB_PAD - HY_EMB), (0, 0)))
    F = w1.shape[1]
    CC = w4.shape[1]
    deltas = jnp.abs(jnp.linspace(math.log(HY_DECAY_TARGET) / HY_SLOW_DECAY,
                                  math.log(HY_DECAY_TARGET) / HY_FAST_DECAY, C, dtype=F32))
    dl = jnp.tile(deltas, CC // C).reshape(1, CC)
    tl = min(L, 1024)
    tn = 1024
    full = lambda shape: pl.BlockSpec(shape, lambda j, i: (0, 0))
    h, s = pl.pallas_call(
        _hy_filter_kernel,
        grid=(CC // tn, L // tl),
        in_specs=[pl.BlockSpec((tl, HY_EMB_PAD), lambda j, i: (i, 0)),
                  full((HY_EMB_PAD, F)), full((1, F)), full((F, F)), full((1, F)), full((F, F)), full((1, F)),
                  pl.BlockSpec((F, tn), lambda j, i: (0, j)), pl.BlockSpec((1, tn), lambda j, i: (0, j))],
        out_specs=[pl.BlockSpec((tl, tn), lambda j, i: (i, j)), pl.BlockSpec((1, tn), lambda j, i: (0, j))],
        out_shape=[jax.ShapeDtypeStruct((L, CC), F32), jax.ShapeDtypeStruct((1, CC), F32)],
        compiler_params=_params("arbitrary", "arbitrary"),
        name="hy_filter",
    )(emb, w1p, b1.reshape(1, F), w2, b2.reshape(1, F), w3, b3.reshape(1, F), w4, dl)
    tot = s.reshape(HY_ORDER, 2, C).sum(axis=1, keepdims=True) + NORM_EPS
    inv = (1.0 / jnp.broadcast_to(tot, (HY_ORDER, 2, C))).reshape(1, CC)
    return h, inv


def dft_tables(L):
    n = 2 * L
    N1 = n // HY_N2
    A = N1 // 2
    NC = min(N1, -(-(A + 1) // 8) * 8)
    c = jnp.arange(NC, dtype=jnp.int32)
    a = jnp.arange(A, dtype=jnp.int32)
    ang1 = ((c[:, None] * a[None, :]) % N1).astype(F32) * (2.0 * math.pi / N1)
    fs = jnp.concatenate([jnp.cos(ang1), -jnp.sin(ang1)], axis=0)
    wr = jnp.where((c == 0) | (c == A), 1.0, jnp.where(c < A, 2.0, 0.0))[:, None]
    fi = jnp.concatenate([wr * jnp.cos(ang1), -wr * jnp.sin(ang1)], axis=0).T
    b = jnp.arange(HY_N2, dtype=jnp.int32)
    d = jnp.arange(HY_N2, dtype=jnp.int32)
    k2 = (b[None, None, :] * (c[:, None, None] + N1 * d[None, :, None])) % n
    ang2 = k2.astype(F32) * (2.0 * math.pi / n)
    cs, sn = jnp.cos(ang2), jnp.sin(ang2)
    gf = jnp.concatenate([jnp.concatenate([cs, sn], axis=2),
                          jnp.concatenate([-sn, cs], axis=2)], axis=1)
    return fs.astype(BF16), fi.astype(BF16), gf.astype(BF16), gf.transpose(0, 2, 1).astype(BF16)


DFT_SUB = 8


def _time_view(x):
    K, B, T, C = x.shape
    return x.reshape(K, B, T // HY_N2, HY_N2 // DFT_SUB, DFT_SUB, C)


LANES = 128


def _strided_rows(refs, first, count, stride):
    return jnp.concatenate([r[pl.ds(first, count, stride=stride), :] for r in refs], axis=1)


def _lane_tile_specs(block, index, n_tiles):
    def spec(t):
        def idx(*g):
            *lead, j = index(*g)
            return (*lead, j * n_tiles + t)
        return pl.BlockSpec(block + (LANES,), idx)
    return [spec(t) for t in range(n_tiles)]


def _dft_s1_kernel(f_ref, *refs):
    *x_refs, o_ref = refs
    A = x_refs[0].shape[0]
    flat = [r.reshape(A * DFT_SUB, LANES) for r in x_refs]
    NC = o_ref.shape[3]
    for bl in range(DFT_SUB):
        y = _dot(f_ref[...], _strided_rows(flat, bl, A, DFT_SUB))
        o_ref[0, 0, bl] = y[:NC]
        o_ref[0, 1, bl] = y[NC:]


def dft_s1(fs, src, k, tc):
    _, B, _, C = src.shape
    M, A = fs.shape
    nt = tc // LANES
    x_specs = _lane_tile_specs((None, None, A, None, DFT_SUB), lambda b, j, g: (k, b, 0, g, 0, j), nt)
    return pl.pallas_call(
        _dft_s1_kernel,
        grid=(B, C // tc, HY_N2 // DFT_SUB),
        in_specs=[pl.BlockSpec((M, A), lambda b, j, g: (0, 0))] + x_specs,
        out_specs=pl.BlockSpec((1, 2, DFT_SUB, M // 2, tc), lambda b, j, g: (b, 0, g, 0, j)),
        out_shape=jax.ShapeDtypeStruct((B, 2, HY_N2, M // 2, C), F32),
        compiler_params=_params("arbitrary", "arbitrary", "arbitrary"),
        name="dft_s1",
    )(fs, *([_time_view(src)] * nt))


def _pair_rows(refs):
    return [r.reshape(2 * r.shape[1] * DFT_SUB, LANES) for r in refs]


def _gather_pair(flat, n_rows, sub):
    return jnp.concatenate([_strided_rows(flat, sub, n_rows, DFT_SUB),
                            _strided_rows(flat, n_rows * DFT_SUB + sub, n_rows, DFT_SUB)], axis=0)


def _dft_mid_kernel(gf_ref, gi_ref, *refs):
    *y_refs, h_ref, o_ref = refs
    flat = _pair_rows(y_refs)
    group = 4
    for c0 in range(0, DFT_SUB, group):
        cls = range(c0, c0 + group)
        xs = [_dot(gf_ref[cl], _gather_pair(flat, HY_N2, cl)) for cl in cls]
        ps = []
        for cl, x in zip(cls, xs):
            xr, xi = x[:HY_N2], x[HY_N2:]
            hr, hi = h_ref[0, 0, 0, cl], h_ref[0, 1, 0, cl]
            ps.append(jnp.concatenate([xr * hr - xi * hi, xr * hi + xi * hr], axis=0).astype(BF16))
        qs = [_dot(gi_ref[cl], p) for cl, p in zip(cls, ps)]
        for cl, q in zip(cls, qs):
            o_ref[0, 0, cl] = q[:HY_N2]
            o_ref[0, 1, cl] = q[HY_N2:]


def dft_mid(gf, gi, y, hs, order, tc):
    B, _, N2, NC, C = y.shape
    nt = tc // LANES
    y6 = y.reshape(B, 2, N2, NC // DFT_SUB, DFT_SUB, C)
    hs6 = hs.reshape(HY_ORDER, 2, NC // DFT_SUB, DFT_SUB, N2, C)
    y_specs = _lane_tile_specs((None, 2, N2, None, DFT_SUB), lambda c, j, b: (b, 0, 0, c, 0, j), nt)
    return pl.pallas_call(
        _dft_mid_kernel,
        grid=(NC // DFT_SUB, C // tc, B),
        in_specs=[pl.BlockSpec((DFT_SUB, 2 * N2, 2 * N2), lambda c, j, b: (c, 0, 0)),
                  pl.BlockSpec((DFT_SUB, 2 * N2, 2 * N2), lambda c, j, b: (c, 0, 0))] + y_specs +
                 [pl.BlockSpec((1, 2, 1, DFT_SUB, N2, tc), lambda c, j, b: (order, 0, c, 0, 0, j))],
        out_specs=pl.BlockSpec((1, 2, DFT_SUB, N2, tc), lambda c, j, b: (b, 0, c, 0, j)),
        out_shape=jax.ShapeDtypeStruct((B, 2, NC, N2, C), F32),
        compiler_params=_params("arbitrary", "arbitrary", "arbitrary"),
        name="dft_mid",
    )(gf, gi, *([y6] * nt), hs6)


def _dft_filt_kernel(gf_ref, *refs):
    *y_refs, inv_ref, o_ref = refs
    nt = len(y_refs) // 2
    flat_f, flat_b = _pair_rows(y_refs[:nt]), _pair_rows(y_refs[nt:])
    inv = inv_ref[...]
    for cl in range(DFT_SUB):
        xf = _dot(gf_ref[cl], _gather_pair(flat_f, HY_N2, cl))
        xb = _dot(gf_ref[cl], _gather_pair(flat_b, HY_N2, cl))
        o_ref[0, 0, 0, cl] = (xf[:HY_N2] + xb[:HY_N2]) * inv
        o_ref[0, 1, 0, cl] = (xf[HY_N2:] - xb[HY_N2:]) * inv


def dft_filter_spectrum(gf, y, inv, C, tc):
    _, _, N2, NC, CC = y.shape
    nj = C // tc
    nt = tc // LANES
    y6 = y.reshape(1, 2, N2, NC // DFT_SUB, DFT_SUB, CC)
    block = (None, 2, N2, None, DFT_SUB)
    yf_specs = _lane_tile_specs(block, lambda c, o, j: (0, 0, 0, c, 0, (2 * o) * nj + j), nt)
    yb_specs = _lane_tile_specs(block, lambda c, o, j: (0, 0, 0, c, 0, (2 * o + 1) * nj + j), nt)
    out = pl.pallas_call(
        _dft_filt_kernel,
        grid=(NC // DFT_SUB, HY_ORDER, nj),
        in_specs=[pl.BlockSpec((DFT_SUB, 2 * N2, 2 * N2), lambda c, o, j: (c, 0, 0))] + yf_specs + yb_specs +
                 [pl.BlockSpec((1, tc), lambda c, o, j: (0, (2 * o) * nj + j))],
        out_specs=pl.BlockSpec((1, 2, 1, DFT_SUB, N2, tc), lambda c, o, j: (o, 0, c, 0, 0, j)),
        out_shape=jax.ShapeDtypeStruct((HY_ORDER, 2, NC // DFT_SUB, DFT_SUB, N2, C), F32),
        compiler_params=_params("arbitrary", "arbitrary", "arbitrary"),
        name="dft_filter_spectrum",
    )(gf, *([y6] * (2 * nt)), inv)
    return out.reshape(HY_ORDER, 2, NC, N2, C)


def _dft_i2_kernel(f_ref, *refs, inv_n):
    nt = (len(refs) - 2) // 3
    q_refs, z_refs, g_refs = refs[:nt], refs[nt:2 * nt], refs[2 * nt:3 * nt]
    bias_ref, o_ref = refs[3 * nt:]
    A = z_refs[0].shape[0]
    NC = q_refs[0].shape[1]
    q_flat = _pair_rows(q_refs)
    z_flat = [r.reshape(A * DFT_SUB, LANES) for r in z_refs]
    g_flat = [r.reshape(A * DFT_SUB, LANES) for r in g_refs]
    for bl in range(DFT_SUB):
        y = _dot(f_ref[...], _gather_pair(q_flat, NC, bl))
        z = _strided_rows(z_flat, bl, A, DFT_SUB)
        g = _strided_rows(g_flat, bl, A, DFT_SUB)
        o_ref[:, 0, bl, :] = g * (y * inv_n + bias_ref[...] * z)


def dft_i2(fi, q, zsrc, kz, gsrc, kg, bias, tc):
    A, M = fi.shape
    B, _, NC, _, C = q.shape
    ng = HY_N2 // DFT_SUB
    nt = tc // LANES
    q6 = q.reshape(B, 2, NC, ng, DFT_SUB, C)
    q_specs = _lane_tile_specs((None, 2, NC, None, DFT_SUB), lambda b, j, g: (b, 0, 0, g, 0, j), nt)
    tspecs = lambda k: _lane_tile_specs((None, None, A, None, DFT_SUB), lambda b, j, g: (k, b, 0, g, 0, j), nt)
    out = pl.pallas_call(
        functools.partial(_dft_i2_kernel, inv_n=1.0 / (2 * A * HY_N2)),
        grid=(B, C // tc, ng),
        in_specs=[pl.BlockSpec((A, M), lambda b, j, g: (0, 0))] + q_specs + tspecs(kz) + tspecs(kg) +
                 [pl.BlockSpec((1, tc), lambda b, j, g: (0, j))],
        out_specs=pl.BlockSpec((None, A, 1, DFT_SUB, tc), lambda b, j, g: (b, 0, g, 0, j)),
        out_shape=jax.ShapeDtypeStruct((B, A, ng, DFT_SUB, C), F32),
        compiler_params=_params("arbitrary", "arbitrary", "arbitrary"),
        name="dft_i2",
    )(fi, *([q6] * nt), *([_time_view(zsrc)] * nt), *([_time_view(gsrc)] * nt), bias)
    return out.reshape(B, A * HY_N2, C)


def _hy_ctx_kernel(fd_ref, fdi_ref, pc_ref, h_ref, bias_ref, o_ref, *, n):
    hp = lax.Precision.HIGHEST
    dot = lambda a, b: jnp.dot(a, b, preferred_element_type=F32, precision=hp)
    fd = fd_ref[...]
    z = pc_ref[0, 0]
    for o in range(HY_ORDER):
        hf = dot(fd, h_ref[2 * o])
        hb = dot(fd, h_ref[2 * o + 1])
        hr = hf[:n] + hb[:n]
        hi = hf[n:] - hb[n:]
        x = dot(fd, z)
        xr, xi = x[:n], x[n:]
        p = jnp.concatenate([xr * hr - xi * hi, xr * hi + xi * hr], axis=0)
        y = dot(fdi_ref[...], p) * (1.0 / n)
        z = pc_ref[o + 1, 0] * (y + bias_ref[o:o + 1] * z)
    o_ref[0] = z


def hy_ctx(pc, filt_ctx, hy_bias, S):
    _, B, T, C = pc.shape
    Lc = T - S
    n = 2 * Lc
    f = jnp.arange(n, dtype=jnp.int32)
    t = jnp.arange(Lc, dtype=jnp.int32)
    ang = ((f[:, None] * t[None, :]) % n).astype(F32) * (2.0 * math.pi / n)
    fd = jnp.concatenate([jnp.cos(ang), -jnp.sin(ang)], axis=0)
    h4 = filt_ctx.reshape(Lc, 2 * HY_ORDER, C).transpose(1, 0, 2)
    tc = 256
    return pl.pallas_call(
        functools.partial(_hy_ctx_kernel, n=n),
        grid=(B, C // tc),
        in_specs=[pl.BlockSpec((2 * n, Lc), lambda b, j: (0, 0)),
                  pl.BlockSpec((Lc, 2 * n), lambda b, j: (0, 0)),
                  pl.BlockSpec((HY_ORDER + 1, 1, Lc, tc), lambda b, j: (0, b, S // Lc, j)),
                  pl.BlockSpec((2 * HY_ORDER, Lc, tc), lambda b, j: (0, 0, j)),
                  pl.BlockSpec((HY_ORDER, tc), lambda b, j: (0, j))],
        out_specs=pl.BlockSpec((1, Lc, tc), lambda b, j: (b, 0, j)),
        out_shape=jax.ShapeDtypeStruct((B, Lc, C), F32),
        compiler_params=_params("arbitrary", "arbitrary"),
        name="hy_ctx",
    )(fd, fd.T, pc, h4, hy_bias)


def hyena_latent(pc, filt, hy_bias, S, tabs):
    fs, fi, gf, gi = tabs
    _, B, T, C = pc.shape
    h_raw, inv = filt
    hy = dft_s1(fs, h_raw.reshape(1, 1, S, 2 * HY_ORDER * C), 0, 512)
    hs = dft_filter_spectrum(gf, hy, inv, C, 512)
    zsrc, kz = pc, 0
    for o in range(HY_ORDER):
        y = dft_s1(fs, zsrc, kz, 512)
        q = dft_mid(gf, gi, y, hs, o, 512)
        z = dft_i2(fi, q, zsrc, kz, pc, o + 1, hy_bias[o].reshape(1, C), 512)
        zsrc, kz = z.reshape(1, B, S, C), 0
    return z


def _flat(a):
    return a.reshape(a.shape[0] * a.shape[1], a.shape[2])


def _proj(h, w, out_dtype, tn):
    B, T, D = h.shape
    M = B * T
    tm = 1024 if M % 1024 == 0 else ROW_TILE
    return mm(_flat(h), w, out_dtype, tm, tn).reshape(B, T, w.shape[1])


def even_layer(h, xs, modtab, gain_ffn, S, w_in, w_out, rpb, gk_up, gk_bias, gnorm):
    B, T, D = xs.shape
    na_w = w_in.shape[1] - 2 * GLA_HEADS * GLA_DK - 2 * (D // 2) - 2 * GLA_LOWRANK
    na_w //= 3
    qk_w = GLA_HEADS * GLA_DK
    v_w = D // 2
    dv = v_w // GLA_HEADS
    o = 3 * na_w
    w_na = w_in[:, :o].astype(BF16)
    wq, wk = w_in[:, o:o + qk_w], w_in[:, o + qk_w:o + 2 * qk_w]
    wv = w_in[:, o + 2 * qk_w:o + 2 * qk_w + v_w]
    wg = w_in[:, o + 2 * qk_w + v_w:o + 2 * qk_w + 2 * v_w]
    wlr = w_in[:, o + 2 * qk_w + 2 * v_w:]
    per_head = lambda w, width: w.reshape(D, GLA_HEADS, width)
    w_gl = jnp.concatenate([per_head(wq, GLA_DK), per_head(_rot_partner_cols(wq), GLA_DK),
                            per_head(wk, GLA_DK), per_head(_rot_partner_cols(wk), GLA_DK),
                            per_head(wv, dv)], axis=-1).reshape(D, GLA_HEADS * GLA_HEAD_COLS).astype(BF16)
    w_g = wg.astype(BF16)
    w_lr = jnp.pad(wlr, ((0, 0), (0, 128 - 2 * GLA_LOWRANK))).astype(BF16)

    qkv = _proj(h, w_na, BF16, 1024)
    gl = _proj(h, w_gl, F32, 768)
    g = _proj(h, w_g, F32, 1024)
    lr = _proj(h, w_lr, F32, 128)

    ya = na_attention(qkv, na_bias_table(rpb, S), S)
    cos_tab, sin_tab = rope_tables(S, T)
    up_tied = _tie_pairs(gk_up.astype(F32)).transpose(2, 0, 1, 3)
    up_tied = jnp.stack([jnp.pad(up_tied[:, 0], ((0, 0), (0, 128 - GLA_LOWRANK), (0, 0))),
                         jnp.pad(up_tied[:, 1], ((0, 0), (GLA_LOWRANK, 128 - 2 * GLA_LOWRANK), (0, 0)))],
                        axis=1)
    bias_tied = _tie_pairs(gk_bias.astype(F32)).transpose(1, 0, 2)[:, :, None, :]
    o_f, o_b = gla_scan(gl, lr, cos_tab, sin_tab, up_tied, bias_tied, S)
    yb = gla_finish(o_f, o_b, g, gnorm)
    w_out_b = w_out.astype(BF16)
    return mm_res([ya, yb], [w_out_b[:na_w], w_out_b[na_w:]], xs, modtab, gain_ffn, S)


def odd_layer(h, xs, modtab, gain_ffn, S, ctx_live, w_in, w_out, hy_short, hyp, hy_bias, sc_conv, tabs):
    B, T, D = xs.shape
    C = D // 2
    n_hy = (HY_ORDER + 1) * C
    w = w_in.astype(BF16)
    pc = hy_pre(_proj(h, w[:, :n_hy], F32, 1024), hy_short, S)
    y_sc = sc_mix(_proj(h, w[:, n_hy:], F32, 1024), sc_conv, S)
    z_lat = hyena_latent(pc, hyena_filters(S, C, *hyp), hy_bias, S, tabs)
    if ctx_live:
        h_ctx, inv_ctx = hyena_filters(T - S, C, *hyp)
        z_ctx = hy_ctx(pc, h_ctx * inv_ctx, hy_bias, S)
    else:
        z_ctx = jnp.zeros((B, T - S, C), F32)
    z = jnp.concatenate([z_lat, z_ctx], axis=1)
    w_out_b = w_out.astype(BF16)
    return mm_res([z, y_sc], [w_out_b[:C], w_out_b[C:]], xs, modtab, gain_ffn, S)


def kernel(x, c, ctx, c_ctx, w_mod, b_mod, norm_mix, norm_ffn, norm_final, w_in_even, w_out_even, na_rpb,
           gla_gk_up, gla_gk_bias, gla_norm, w_in_odd, w_out_odd, hy_short, hy_w1, hy_b1, hy_w2, hy_b2,
           hy_w3, hy_b3, hy_w4, hy_bias, sc_conv, moe_w_group, moe_b_group, moe_w_expert, moe_b_expert,
           moe_w1, moe_w3, moe_w2):
    B, S, D = x.shape
    Lc = ctx.shape[1]
    T = S + Lc
    depth = w_mod.shape[0]
    last_even = 2 * ((depth - 1) // 2)
    xs = jnp.concatenate([x, ctx], axis=1)
    tabs = dft_tables(S)

    cvec = jnp.concatenate([c, c_ctx[None, :], jnp.zeros((8 - B - 1, D), F32)], axis=0)
    cvec = cvec * jax.nn.sigmoid(cvec)
    modtabs = []
    for l in range(depth):
        mod = mm(cvec, w_mod, F32, 8, 1024, layer=l)[:B + 1] + b_mod[l]
        mod = mod.reshape(B + 1, 6, D)
        cx = jnp.broadcast_to(mod[B][None], (B, 6, D))
        modtabs.append(jnp.pad(jnp.stack([mod[:B], cx], axis=1), ((0, 0), (0, 0), (0, 2), (0, 0))))

    h = normmod(xs, norm_mix[0], modtabs[0], MOD_MIX, S, BF16)
    for l in range(depth):
        modtab = modtabs[l]
        if l % 2 == 0:
            e = l // 2
            xs, tok = even_layer(h, xs, modtab, norm_ffn[l], S, w_in_even[e], w_out_even[e], na_rpb[e],
                                 gla_gk_up[e], gla_gk_bias[e], gla_norm[e])
        else:
            o = l // 2
            hyp = (hy_w1[o], hy_b1[o], hy_w2[o], hy_b2[o], hy_w3[o], hy_b3[o], hy_w4[o])
            xs, tok = odd_layer(h, xs, modtab, norm_ffn[l], S, l < last_even, w_in_odd[o], w_out_odd[o],
                                hy_short[o], hyp, hy_bias[o], sc_conv[o], tabs)

        w_router = jnp.pad(jnp.concatenate([moe_w_group[l], moe_w_expert[l]], axis=1),
                           ((0, 0), (0, 128 - N_GROUPS - N_EXPERTS))).astype(BF16)
        b_router = jnp.concatenate([moe_b_group[l], moe_b_expert[l]])
        pair = hier_moe(_flat(tok), w_router, b_router, l, moe_w1, moe_w3, moe_w2)
        if l + 1 < depth:
            xs, h = moe_combine(xs, pair, modtab, norm_mix[l + 1], modtabs[l + 1], S, False)
        else:
            (out,) = moe_combine(xs, pair, modtab, norm_final, modtab, S, True)
    return out
```

```python
import functools
import math

import jax
import jax.numpy as jnp
from jax import lax
from jax.experimental import pallas as pl
from jax.experimental.pallas import tpu as pltpu

F32 = jnp.float32
BF16 = jnp.bfloat16

NORM_EPS = 1e-6
NEG_INF = -1e30
ROPE_THETA = 10000.0
GRID_W = 64
HEAD_DIM = 128
NA_WIN_ROWS = 8
NA_WIN_COLS = 16
GLA_HEADS = 4
GLA_DK = 128
GLA_LOWRANK = 16
GLA_GATE_NORMALIZER = 16.0
GLA_CHUNK = 64
HY_ORDER = 2
HY_EMB = 33
HY_SIN_FREQ = 1.0
HY_DECAY_TARGET = 1e-2
HY_FAST_DECAY = 0.3
HY_SLOW_DECAY = 1.5
CONV_W = 3
N_GROUPS = 4
EXPERTS_PER_GROUP = 8
N_EXPERTS = N_GROUPS * EXPERTS_PER_GROUP
TOP_K_IN_GROUP = 2

ROW_TILE = 256
NA_GROUP_ROWS = 4
NA_SPAN_ROWS = NA_GROUP_ROWS + NA_WIN_ROWS
MOE_TILE = 512
VMEM_LIMIT = 56 * 1024 * 1024


def _params(*sem):
    return pltpu.CompilerParams(dimension_semantics=sem, vmem_limit_bytes=VMEM_LIMIT)


def _dot(a, b):
    return jnp.dot(a.astype(BF16), b.astype(BF16), preferred_element_type=F32)


def _dot_nt(a, b):
    return lax.dot_general(a.astype(BF16), b.astype(BF16), (((1,), (1,)), ((), ())),
                           preferred_element_type=F32)


def _mm_kernel(a_ref, w_ref, o_ref):
    o_ref[...] = _dot(a_ref[...], w_ref[...]).astype(o_ref.dtype)


def mm(a, w, out_dtype, tm, tn, layer=None):
    M, K = a.shape
    N = w.shape[-1]
    assert M % tm == 0 and N % tn == 0, (M, N, tm, tn)
    if layer is None:
        w_spec = pl.BlockSpec((K, tn), lambda j, i: (0, j))
    else:
        w_spec = pl.BlockSpec((None, K, tn), lambda j, i: (layer, 0, j))
    return pl.pallas_call(
        _mm_kernel,
        grid=(N // tn, M // tm),
        in_specs=[pl.BlockSpec((tm, K), lambda j, i: (i, 0)), w_spec],
        out_specs=pl.BlockSpec((tm, tn), lambda j, i: (i, j)),
        out_shape=jax.ShapeDtypeStruct((M, N), out_dtype),
        compiler_params=_params("arbitrary", "arbitrary"),
        name="mm",
    )(a, w)


MOD_MIX = (0, 1, 2)
MOD_FFN = (3, 4, 5)


def _norm_modulate(x, gain, mod, rows):
    y = x * lax.rsqrt(jnp.mean(x * x, axis=-1, keepdims=True) + NORM_EPS) * gain
    return y * (1.0 + mod[rows[1]:rows[1] + 1]) + mod[rows[0]:rows[0] + 1]


def _normmod_kernel(x_ref, g_ref, mod_ref, o_ref, *, rows):
    o_ref[0] = _norm_modulate(x_ref[0], g_ref[...], mod_ref[0, 0], rows).astype(o_ref.dtype)


def normmod(xs, gain, modtab, rows, S, out_dtype):
    B, T, D = xs.shape
    n_lat = S // ROW_TILE
    return pl.pallas_call(
        functools.partial(_normmod_kernel, rows=rows),
        grid=(B, T // ROW_TILE),
        in_specs=[pl.BlockSpec((1, ROW_TILE, D), lambda b, i: (b, i, 0)),
                  pl.BlockSpec((1, D), lambda b, i: (0, 0)),
                  pl.BlockSpec((1, 1, 8, D), lambda b, i: (b, jnp.where(i >= n_lat, 1, 0), 0, 0))],
        out_specs=pl.BlockSpec((1, ROW_TILE, D), lambda b, i: (b, i, 0)),
        out_shape=jax.ShapeDtypeStruct((B, T, D), out_dtype),
        compiler_params=_params("arbitrary", "arbitrary"),
        name="normmod",
    )(xs, gain.reshape(1, D), modtab)


def _gla_output(of_ref, ob_ref, g_ref, gn_ref):
    dv = gn_ref.shape[-1]
    gn = gn_ref[...]
    parts = []
    for h in range(of_ref.shape[-1] // dv):
        sl = slice(h * dv, (h + 1) * dv)
        o = of_ref[0, :, sl] + ob_ref[0, :, sl]
        o = o * lax.rsqrt(jnp.mean(o * o, axis=-1, keepdims=True) + NORM_EPS) * gn
        g = g_ref[0, :, sl]
        parts.append((o * (g * jax.nn.sigmoid(g))).astype(BF16))
    return jnp.concatenate(parts, axis=-1)


def _mm_res_kernel(*refs, n_a, n_gla):
    a_refs = refs[:n_a]
    gla_refs = refs[n_a:n_a + n_gla]
    n_w = n_a + (1 if n_gla else 0)
    w_refs = refs[n_a + n_gla:n_a + n_gla + n_w]
    res_ref, mod_ref, gain_ref, o_ref, t_ref = refs[n_a + n_gla + n_w:]
    a_vals = [a_ref[0] for a_ref in a_refs] + ([_gla_output(*gla_refs)] if n_gla else [])
    acc = _dot(a_vals[0], w_refs[0][...])
    for a, w_ref in zip(a_vals[1:], w_refs[1:]):
        acc = acc + _dot(a, w_ref[...])
    mod = mod_ref[0, 0]
    x = res_ref[0] + mod[MOD_MIX[2]:MOD_MIX[2] + 1] * acc
    o_ref[0] = x
    t_ref[0] = _norm_modulate(x, gain_ref[...], mod, MOD_FFN).astype(t_ref.dtype)


def mm_res(a_list, w_list, res, modtab, gain_ffn, S, gla=None):
    B, T, D = res.shape
    n_lat = S // ROW_TILE
    n_a = len(a_list)
    row = pl.BlockSpec((1, ROW_TILE, D), lambda b, i: (b, i, 0))
    arow = lambda a: pl.BlockSpec((1, ROW_TILE, a.shape[-1]), lambda b, i: (b, i, 0))
    in_specs = [arow(a) for a in a_list]
    gla_args = []
    if gla is not None:
        o_f, o_b, g, gnorm = gla
        gla_args = [o_f, o_b, g, gnorm.reshape(1, -1)]
        in_specs += [arow(o_f), arow(o_b), arow(g), pl.BlockSpec((1, gnorm.shape[0]), lambda b, i: (0, 0))]
    in_specs += [pl.BlockSpec(w.shape, lambda b, i: (0, 0)) for w in w_list]
    in_specs += [row, pl.BlockSpec((1, 1, 8, D), lambda b, i: (b, jnp.where(i >= n_lat, 1, 0), 0, 0)),
                 pl.BlockSpec((1, D), lambda b, i: (0, 0))]
    return pl.pallas_call(
        functools.partial(_mm_res_kernel, n_a=n_a, n_gla=len(gla_args)),
        grid=(B, T // ROW_TILE),
        in_specs=in_specs,
        out_specs=[row, row],
        out_shape=[jax.ShapeDtypeStruct((B, T, D), F32), jax.ShapeDtypeStruct((B, T, D), BF16)],
        compiler_params=_params("arbitrary", "arbitrary"),
        name="mm_res",
    )(*a_list, *gla_args, *w_list, res, modtab, gain_ffn.reshape(1, D))


def _moe_combine_kernel(x_ref, p_ref, mod_ref, gain_ref, modn_ref, *out_refs, final):
    f = p_ref[0, 0].astype(F32) + p_ref[1, 0].astype(F32)
    x = x_ref[0] + mod_ref[0, 0][MOD_FFN[2]:MOD_FFN[2] + 1] * f
    if final:
        out_refs[0][0] = x * lax.rsqrt(jnp.mean(x * x, axis=-1, keepdims=True) + NORM_EPS) * gain_ref[...]
    else:
        out_refs[0][0] = x
        out_refs[1][0] = _norm_modulate(x, gain_ref[...], modn_ref[0, 0], MOD_MIX).astype(out_refs[1].dtype)


def moe_combine(xs, pair, modtab, gain_next, modtab_next, S, final):
    B, T, D = xs.shape
    n_lat = S // ROW_TILE
    rows = S if final else T
    row = pl.BlockSpec((1, ROW_TILE, D), lambda b, i: (b, i, 0))
    mspec = pl.BlockSpec((1, 1, 8, D), lambda b, i: (b, jnp.where(i >= n_lat, 1, 0), 0, 0))
    if final:
        out_specs, out_shape = [row], [jax.ShapeDtypeStruct((B, S, D), F32)]
    else:
        out_specs = [row, row]
        out_shape = [jax.ShapeDtypeStruct((B, T, D), F32), jax.ShapeDtypeStruct((B, T, D), BF16)]
    return pl.pallas_call(
        functools.partial(_moe_combine_kernel, final=final),
        grid=(B, rows // ROW_TILE),
        in_specs=[row, pl.BlockSpec((2, 1, ROW_TILE, D), lambda b, i: (0, b, i, 0)), mspec,
                  pl.BlockSpec((1, D), lambda b, i: (0, 0)), mspec],
        out_specs=out_specs,
        out_shape=out_shape,
        compiler_params=_params("arbitrary", "arbitrary"),
        name="moe_combine",
    )(xs, pair.reshape(2, B, T, D), modtab, gain_next.reshape(1, D), modtab_next)


NA_HEADS_PER_STEP = 4


def _na_kernel(q_ref, k_ref, v_ref, bias_ref, o_ref, *, S, Lc, n_lat):
    i = pl.program_id(2)
    rows = S // GRID_W
    scale = HEAD_DIM ** -0.5
    heads = [slice(h * HEAD_DIM, (h + 1) * HEAD_DIM) for h in range(NA_HEADS_PER_STEP)]
    q = [q_ref[0, :, hs] for hs in heads]
    vc = [v_ref[0, pl.ds(S, Lc), hs] for hs in heads]
    s_ctx = [_dot_nt(q[h], k_ref[0, pl.ds(S, Lc), hs]) * scale for h, hs in enumerate(heads)]

    @pl.when(i < n_lat)
    def _():
        ks = jnp.clip(NA_GROUP_ROWS * i - NA_WIN_ROWS // 2, 0, rows - NA_SPAN_ROWS)
        start = pl.multiple_of(ks * GRID_W, GRID_W)
        span = pl.ds(start, NA_SPAN_ROWS * GRID_W)
        s_loc = [_dot_nt(q[h], k_ref[0, span, hs]) * scale + bias_ref[h, 0] for h, hs in enumerate(heads)]
        m = [jnp.maximum(jnp.max(s_loc[h], axis=-1, keepdims=True), jnp.max(s_ctx[h], axis=-1, keepdims=True))
             for h in range(NA_HEADS_PER_STEP)]
        p_loc = [jnp.exp(s_loc[h] - m[h]) for h in range(NA_HEADS_PER_STEP)]
        p_ctx = [jnp.exp(s_ctx[h] - m[h]) for h in range(NA_HEADS_PER_STEP)]
        for h, hs in enumerate(heads):
            den = jnp.sum(p_loc[h], axis=-1, keepdims=True) + jnp.sum(p_ctx[h], axis=-1, keepdims=True)
            o = _dot(p_loc[h], v_ref[0, span, hs]) + _dot(p_ctx[h], vc[h])
            o_ref[0, :, hs] = (o / den).astype(o_ref.dtype)

    @pl.when(i >= n_lat)
    def _():
        for h, hs in enumerate(heads):
            m = jnp.max(s_ctx[h], axis=-1, keepdims=True)
            p = jnp.exp(s_ctx[h] - m)
            o = _dot(p, vc[h])
            o_ref[0, :, hs] = (o / jnp.sum(p, axis=-1, keepdims=True)).astype(o_ref.dtype)


def na_bias_table(rpb, S):
    rows = S // GRID_W
    n_lat = S // ROW_TILE
    kc = NA_WIN_COLS
    cols = jnp.arange(GRID_W)
    col_start = jnp.clip(cols - kc // 2, 0, GRID_W - kc)
    col_ok = (cols[None, :] >= col_start[:, None]) & (cols[None, :] < col_start[:, None] + kc)
    coff = jnp.clip(cols[None, :] - cols[:, None], -(kc - 1), kc - 1) + kc - 1
    cb = jnp.take(rpb.astype(F32), coff, axis=-1)
    cb = jnp.where(col_ok[None, None], cb, NEG_INF)
    g = jnp.array([0, min(1, n_lat - 1), n_lat - 1])
    ks = jnp.clip(NA_GROUP_ROWS * g - NA_WIN_ROWS // 2, 0, rows - NA_SPAN_ROWS)
    r = NA_GROUP_ROWS * g[:, None] + jnp.arange(NA_GROUP_ROWS)[None, :]
    k = ks[:, None] + jnp.arange(NA_SPAN_ROWS)[None, :]
    rs = jnp.clip(r - NA_WIN_ROWS // 2, 0, rows - NA_WIN_ROWS)
    valid = (k[:, None, :] >= rs[:, :, None]) & (k[:, None, :] < rs[:, :, None] + NA_WIN_ROWS)
    delta = jnp.clip(k[:, None, :] - r[:, :, None] + NA_WIN_ROWS - 1, 0, 2 * NA_WIN_ROWS - 2)
    t = cb[:, delta]
    t = jnp.where(valid[None, :, :, :, None, None], t, NEG_INF)
    t = t.transpose(0, 1, 2, 4, 3, 5)
    H = rpb.shape[0]
    return t.reshape(H, 3, NA_GROUP_ROWS * GRID_W, NA_SPAN_ROWS * GRID_W)


def na_attention(qkv, bias_tab, S):
    B, T, W3 = qkv.shape
    H = W3 // (3 * HEAD_DIM)
    Lc = T - S
    n_lat = S // ROW_TILE
    assert Lc == ROW_TILE and NA_GROUP_ROWS * GRID_W == ROW_TILE
    span = NA_SPAN_ROWS * GRID_W

    hps = NA_HEADS_PER_STEP
    hw = hps * HEAD_DIM
    ng = H // hps

    def bias_idx(b, h, i):
        return (h, jnp.where(i == 0, 0, jnp.where(i == n_lat - 1, 2, 1)), 0, 0)

    return pl.pallas_call(
        functools.partial(_na_kernel, S=S, Lc=Lc, n_lat=n_lat),
        grid=(B, ng, n_lat + 1),
        in_specs=[pl.BlockSpec((1, ROW_TILE, hw), lambda b, h, i: (b, i, h)),
                  pl.BlockSpec((1, T, hw), lambda b, h, i: (b, 0, ng + h)),
                  pl.BlockSpec((1, T, hw), lambda b, h, i: (b, 0, 2 * ng + h)),
                  pl.BlockSpec((hps, 1, ROW_TILE, span), bias_idx)],
        out_specs=pl.BlockSpec((1, ROW_TILE, hw), lambda b, h, i: (b, i, h)),
        out_shape=jax.ShapeDtypeStruct((B, T, H * HEAD_DIM), BF16),
        compiler_params=_params("arbitrary", "arbitrary", "arbitrary"),
        name="na_attention",
    )(qkv, qkv, qkv, bias_tab)


GLA_HEAD_COLS = 4 * GLA_DK + 2 * GLA_DK


GLA_HEADS_PER_STEP = 4


def _gla_prep(x, lr, cos, sin, up, gb, reverse):
    R = x.shape[0]
    C = GLA_CHUNK
    dk = GLA_DK
    q = (x[:, 0:dk] * cos + x[:, dk:2 * dk] * sin) * (dk ** -0.5)
    k = x[:, 2 * dk:3 * dk] * cos + x[:, 3 * dk:4 * dk] * sin
    v = x[:, 4 * dk:6 * dk]
    z = _dot(lr, up) + gb
    la = (jnp.minimum(z, 0.0) - jnp.log(1.0 + jnp.exp(-jnp.abs(z)))) / GLA_GATE_NORMALIZER
    ri = lax.broadcasted_iota(jnp.int32, (R, R), 0)
    ci = lax.broadcasted_iota(jnp.int32, (R, R), 1)
    same = (ri // C) == (ci // C)
    cum = jnp.logical_and(same, (ci >= ri) if reverse else (ci <= ri)).astype(BF16)
    la_hi = la.astype(BF16)
    la_lo = la - la_hi.astype(F32)
    b = _dot(cum, la_hi) + _dot(cum, la_lo)
    return q, k, v, b


def _gla_kernel(xf_ref, xb_ref, lrf_ref, lrb_ref, cf_ref, sf_ref, cb_ref, sb_ref, up_ref, gb_ref,
                of_ref, ob_ref, state_ref):
    @pl.when(pl.program_id(2) == 0)
    def _():
        state_ref[...] = jnp.zeros_like(state_ref)

    C = GLA_CHUNK
    dk = GLA_DK
    dv = 2 * dk
    n_chunks = ROW_TILE // C
    rc = lax.broadcasted_iota(jnp.int32, (C, C), 0)
    cc = lax.broadcasted_iota(jnp.int32, (C, C), 1)
    chains = []
    for h in range(GLA_HEADS_PER_STEP):
        cols = slice(h * GLA_HEAD_COLS, (h + 1) * GLA_HEAD_COLS)
        for d, (x_ref, lr_ref, c_ref, s_ref, o_ref) in enumerate(
                ((xf_ref, lrf_ref, cf_ref, sf_ref, of_ref), (xb_ref, lrb_ref, cb_ref, sb_ref, ob_ref))):
            q, k, v, b = _gla_prep(x_ref[0, :, cols], lr_ref[0], c_ref[...], s_ref[...], up_ref[h, d], gb_ref[h, d],
                                   d == 1)
            chains.append(dict(q=q, k=k, v=v, b=b, d=d, h=h, o_ref=o_ref, state=state_ref[d, h]))
    for step in range(n_chunks):
        for ch in chains:
            rev = ch["d"] == 1
            c = n_chunks - 1 - step if rev else step
            sl = slice(c * C, (c + 1) * C)
            mid = C // 2 if rev else C // 2 - 1
            last = 0 if rev else C - 1
            bc, qc, kc = ch["b"][sl], ch["q"][sl], ch["k"][sl]
            b_mid = bc[mid:mid + 1]
            b_last = bc[last:last + 1]
            att = _dot_nt(qc * jnp.exp(bc - b_mid), kc * jnp.exp(b_mid - bc))
            ch.update(sl=sl, att=jnp.where((cc >= rc) if rev else (cc <= rc), att, 0.0),
                      qb=qc * jnp.exp(bc), kd_t=(kc * jnp.exp(b_last - bc)).T,
                      dec=jnp.exp(jnp.broadcast_to(b_last, (dk, dk)).T[:, :1]))
        for ch in chains:
            vc = ch["v"][ch["sl"]]
            o = _dot(ch["att"], vc) + _dot(ch["qb"], ch["state"])
            ch["o_ref"][0, ch["sl"], ch["h"] * dv:(ch["h"] + 1) * dv] = o
            ch["state"] = ch["dec"] * ch["state"] + _dot(ch["kd_t"], vc)
    for ch in chains:
        state_ref[ch["d"], ch["h"]] = ch["state"]


def gla_scan(gl, lr, cos_tab, sin_tab, up_tied, bias_tied, S):
    B, T, W = gl.shape
    H = W // GLA_HEAD_COLS
    n_blk = T // ROW_TILE
    dv = 2 * GLA_DK
    hps = GLA_HEADS_PER_STEP
    blk_f = lambda i: jnp.where(i == 0, n_blk - 1, i - 1)
    blk_b = lambda i: jnp.where(i == 0, n_blk - 1, n_blk - 1 - i)
    xspec = lambda blk: pl.BlockSpec((1, ROW_TILE, hps * GLA_HEAD_COLS), lambda b, h, i: (b, blk(i), h))
    lspec = lambda blk: pl.BlockSpec((1, ROW_TILE, 128), lambda b, h, i: (b, blk(i), 0))
    tspec = lambda blk: pl.BlockSpec((ROW_TILE, GLA_DK), lambda b, h, i: (blk(i), 0))
    ospec = lambda blk: pl.BlockSpec((1, ROW_TILE, hps * dv), lambda b, h, i: (b, blk(i), h))
    out = jax.ShapeDtypeStruct((B, T, H * dv), F32)
    return pl.pallas_call(
        _gla_kernel,
        grid=(B, H // hps, n_blk),
        in_specs=[xspec(blk_f), xspec(blk_b), lspec(blk_f), lspec(blk_b),
                  tspec(blk_f), tspec(blk_f), tspec(blk_b), tspec(blk_b),
                  pl.BlockSpec((hps, 2, 128, GLA_DK), lambda b, h, i: (h, 0, 0, 0)),
                  pl.BlockSpec((hps, 2, 1, GLA_DK), lambda b, h, i: (h, 0, 0, 0))],
        out_specs=[ospec(blk_f), ospec(blk_b)],
        out_shape=[out, out],
        scratch_shapes=[pltpu.VMEM((2, hps, GLA_DK, dv), F32)],
        compiler_params=_params("arbitrary", "arbitrary", "arbitrary"),
        name="gla_scan",
    )(gl, gl, lr, lr, cos_tab, sin_tab, cos_tab, sin_tab, up_tied, bias_tied)


def rope_tables(S, T):
    half = GLA_DK // 2
    nf = half // 2
    pos = jnp.arange(S)
    inv_freq = ROPE_THETA ** (-jnp.arange(nf, dtype=F32) / nf)

    def part(p):
        ang = p.astype(F32)[:, None] * inv_freq[None, :]
        c, s = jnp.cos(ang), jnp.sin(ang)
        return jnp.concatenate([c, c], axis=-1), jnp.concatenate([-s, s], axis=-1)

    cr, sr = part(pos // GRID_W)
    cc, sc = part(pos % GRID_W)
    cos = jnp.concatenate([cr, cc], axis=-1)
    sin = jnp.concatenate([sr, sc], axis=-1)
    cos = jnp.concatenate([cos, jnp.ones((T - S, GLA_DK), F32)], axis=0)
    sin = jnp.concatenate([sin, jnp.zeros((T - S, GLA_DK), F32)], axis=0)
    return cos, sin


def _rot_partner_cols(w):
    K = w.shape[0]
    w4 = w.reshape(K, GLA_HEADS, 4, GLA_DK // 4)
    return w4[:, :, jnp.array([1, 0, 3, 2])].reshape(K, GLA_HEADS * GLA_DK)


def _tie_pairs(u):
    lead = u.shape[:-1]
    u = u.reshape(lead + (GLA_HEADS, 2, 1, GLA_DK // 4))
    u = jnp.broadcast_to(u, lead + (GLA_HEADS, 2, 2, GLA_DK // 4))
    return u.reshape(lead + (GLA_HEADS, GLA_DK))


def _moe_kernel(be_ref, bv_ref, x_ref, w1_ref, w3_ref, w2_ref, g_ref, o_ref, w1b, w3b, w2b):
    i = pl.program_id(0)
    changed = jnp.logical_or(i == 0, be_ref[i] != be_ref[jnp.maximum(i - 1, 0)])

    @pl.when(changed)
    def _():
        w1b[...] = w1_ref[0, 0].astype(BF16)
        w3b[...] = w3_ref[0, 0].astype(BF16)
        w2b[...] = w2_ref[0, 0].astype(BF16)

    @pl.when(bv_ref[i] != 0)
    def _():
        x = x_ref[...]
        h1 = jnp.dot(x, w1b[...], preferred_element_type=F32)
        h3 = jnp.dot(x, w3b[...], preferred_element_type=F32)
        a = (h1 * jax.nn.sigmoid(h1)) * h3
        y = jnp.dot(a.astype(BF16), w2b[...], preferred_element_type=F32)
        g = g_ref[...]
        o_ref[...] = jnp.where(g != 0.0, y * g, 0.0).astype(o_ref.dtype)

    @pl.when(bv_ref[i] == 0)
    def _():
        o_ref[...] = jnp.zeros_like(o_ref)


def moe_experts(xs, layer, w1, w3, w2, slot_gate, block_expert, block_valid):
    cap, D = xs.shape
    De = w1.shape[-1]
    n_blocks = cap // MOE_TILE
    return pl.pallas_call(
        _moe_kernel,
        grid_spec=pltpu.PrefetchScalarGridSpec(
            num_scalar_prefetch=2,
            grid=(n_blocks,),
            in_specs=[pl.BlockSpec((MOE_TILE, D), lambda i, be, bv: (i, 0)),
                      pl.BlockSpec((1, 1, D, De), lambda i, be, bv: (layer, be[i], 0, 0)),
                      pl.BlockSpec((1, 1, D, De), lambda i, be, bv: (layer, be[i], 0, 0)),
                      pl.BlockSpec((1, 1, De, D), lambda i, be, bv: (layer, be[i], 0, 0)),
                      pl.BlockSpec((MOE_TILE, 1), lambda i, be, bv: (i, 0))],
            out_specs=pl.BlockSpec((MOE_TILE, D), lambda i, be, bv: (i, 0)),
            scratch_shapes=[pltpu.VMEM((D, De), BF16), pltpu.VMEM((D, De), BF16), pltpu.VMEM((De, D), BF16)]),
        out_shape=jax.ShapeDtypeStruct((cap, D), BF16),
        compiler_params=_params("arbitrary"),
        name="moe_experts",
    )(block_expert, block_valid, xs, w1, w3, w2, slot_gate)


def _router_kernel(x_ref, w_ref, b_ref, o_ref):
    logits = _dot(x_ref[...], w_ref[...]) + b_ref[...]
    lane = lax.broadcasted_iota(jnp.int32, logits.shape, 1)
    ninf = -jnp.inf

    def first_max(v):
        m = jnp.max(v, axis=-1, keepdims=True)
        return m, jnp.min(jnp.where(v == m, lane, logits.shape[1]), axis=-1, keepdims=True)

    is_g = lane < N_GROUPS
    m_g, grp = first_max(jnp.where(is_g, logits, ninf))
    p_grp = 1.0 / jnp.sum(jnp.where(is_g, jnp.exp(logits - m_g), 0.0), axis=-1, keepdims=True)
    lo = N_GROUPS + EXPERTS_PER_GROUP * grp
    le = jnp.where((lane >= lo) & (lane < lo + EXPERTS_PER_GROUP), logits, ninf)
    m1, i1 = first_max(le)
    m2, i2 = first_max(jnp.where(lane == i1, ninf, le))
    e = jnp.exp(m2 - m1)
    g1 = p_grp * (1.0 / (1.0 + e))
    g2 = p_grp * (e / (1.0 + e))
    out = jnp.where(lane == 0, (i1 - N_GROUPS).astype(F32),
                    jnp.where(lane == 1, (i2 - N_GROUPS).astype(F32),
                              jnp.where(lane == 2, g1, jnp.where(lane == 3, g2, 0.0))))
    o_ref[...] = out


def router(tok, w_router, b_router):
    N, D = tok.shape
    tm = 1024 if N % 1024 == 0 else ROW_TILE
    W = w_router.shape[1]
    return pl.pallas_call(
        _router_kernel,
        grid=(N // tm,),
        in_specs=[pl.BlockSpec((tm, D), lambda i: (i, 0)), pl.BlockSpec((D, W), lambda i: (0, 0)),
                  pl.BlockSpec((1, W), lambda i: (0, 0))],
        out_specs=pl.BlockSpec((tm, W), lambda i: (i, 0)),
        out_shape=jax.ShapeDtypeStruct((N, W), F32),
        compiler_params=_params("arbitrary"),
        name="router",
    )(tok, w_router, b_router)


def hier_moe(tok, w_router, b_router, layer, w1, w3, w2):
    N, D = tok.shape
    r = router(tok, w_router, b_router)
    eid = r[:, :TOP_K_IN_GROUP].astype(jnp.int32)
    gate = r[:, TOP_K_IN_GROUP:2 * TOP_K_IN_GROUP]
    A = N * TOP_K_IN_GROUP
    e_flat = eid.reshape(A)
    onehot = (e_flat[:, None] == jnp.arange(N_EXPERTS, dtype=jnp.int32)[None, :]).astype(jnp.int32)
    csum = jnp.cumsum(onehot, axis=0)
    rank = jnp.take_along_axis(csum, e_flat[:, None], axis=1)[:, 0] - 1
    counts = csum[-1]
    padded = (counts + MOE_TILE - 1) // MOE_TILE * MOE_TILE
    pend = jnp.cumsum(padded)
    pstart = pend - padded
    slot = pstart[e_flat] + rank
    n_blocks = -(-A // MOE_TILE) + N_EXPERTS
    cap = n_blocks * MOE_TILE
    slot_asg = jnp.full((cap,), -1, jnp.int32).at[slot].set(jnp.arange(A, dtype=jnp.int32))
    used = slot_asg >= 0
    slot_tok = jnp.where(used, slot_asg // TOP_K_IN_GROUP, jnp.arange(cap, dtype=jnp.int32) % N)
    slot_gate = jnp.where(used, gate.reshape(A)[jnp.maximum(slot_asg, 0)], 0.0)
    bstart = jnp.arange(n_blocks, dtype=jnp.int32) * MOE_TILE
    block_expert = jnp.minimum(jnp.sum((bstart[:, None] >= pend[None, :]).astype(jnp.int32), axis=1), N_EXPERTS - 1)
    block_valid = (bstart < pend[-1]).astype(jnp.int32)
    xs = tok[slot_tok]
    ys = moe_experts(xs, layer, w1, w3, w2, slot_gate[:, None], block_expert, block_valid)
    return ys[slot.reshape(N, TOP_K_IN_GROUP).T]


CONV_LANES = 128
HY_N2 = 128


def _conv3_time(x, w, S):
    T = x.shape[0]
    t = lax.broadcasted_iota(jnp.int32, x.shape, 0)
    prev = jnp.where((t == 0) | (t == S), 0.0, pltpu.roll(x, 1, 0))
    nxt = jnp.where((t == S - 1) | (t == T - 1), 0.0, pltpu.roll(x, T - 1, 0))
    return w[0:1] * prev + w[1:2] * x + w[2:3] * nxt


def _hy_pre_kernel(p_ref, w_ref, o_ref, *, S):
    o_ref[0, 0] = _conv3_time(p_ref[0], w_ref[...], S)


def hy_pre(p_hy, hy_short, S):
    B, T, C3 = p_hy.shape
    C = C3 // (HY_ORDER + 1)
    nj = C // CONV_LANES
    return pl.pallas_call(
        functools.partial(_hy_pre_kernel, S=S),
        grid=(B, HY_ORDER + 1, nj),
        in_specs=[pl.BlockSpec((1, T, CONV_LANES), lambda b, k, j: (b, 0, k * nj + j)),
                  pl.BlockSpec((CONV_W, CONV_LANES), lambda b, k, j: (0, k * nj + j))],
        out_specs=pl.BlockSpec((1, 1, T, CONV_LANES), lambda b, k, j: (k, b, 0, j)),
        out_shape=jax.ShapeDtypeStruct((HY_ORDER + 1, B, T, C), F32),
        compiler_params=_params("arbitrary", "arbitrary", "arbitrary"),
        name="hy_pre",
    )(p_hy, hy_short)


def _sc_mix_kernel(b_ref, c_ref, x_ref, w_ref, o_ref, *, S):
    o_ref[0] = (b_ref[0] * _conv3_time(c_ref[0] * x_ref[0], w_ref[...], S)).astype(o_ref.dtype)


def sc_mix(p_sc, sc_conv, S):
    B, T, C3 = p_sc.shape
    C = C3 // 3
    nj = C // CONV_LANES
    spec = lambda k: pl.BlockSpec((1, T, CONV_LANES), lambda b, j: (b, 0, k * nj + j))
    return pl.pallas_call(
        functools.partial(_sc_mix_kernel, S=S),
        grid=(B, nj),
        in_specs=[spec(0), spec(1), spec(2), pl.BlockSpec((CONV_W, CONV_LANES), lambda b, j: (0, j))],
        out_specs=pl.BlockSpec((1, T, CONV_LANES), lambda b, j: (b, 0, j)),
        out_shape=jax.ShapeDtypeStruct((B, T, C), BF16),
        compiler_params=_params("arbitrary", "arbitrary"),
        name="sc_mix",
    )(p_sc, p_sc, p_sc, sc_conv)


HY_EMB_PAD = 128


def _hy_filter_kernel(emb_ref, w1_ref, b1_ref, w2_ref, b2_ref, w3_ref, b3_ref, w4_ref, dl_ref, h_ref, s_ref):
    emb = emb_ref[...]
    a = jnp.sin(HY_SIN_FREQ * (_dot(emb, w1_ref[...]) + b1_ref[...]))
    a = jnp.sin(HY_SIN_FREQ * (_dot(a, w2_ref[...]) + b2_ref[...]))
    a = jnp.sin(HY_SIN_FREQ * (_dot(a, w3_ref[...]) + b3_ref[...]))
    t = emb[:, 0:1]
    h = _dot(a, w4_ref[...]) * jnp.exp(-t * dl_ref[...])
    h_ref[...] = h

    @pl.when(pl.program_id(1) == 0)
    def _():
        s_ref[...] = jnp.zeros_like(s_ref)

    s_ref[...] += jnp.sum(jnp.abs(h), axis=0, keepdims=True)


def hyena_filters(L, C, w1, b1, w2, b2, w3, b3, w4):
    t = jnp.linspace(0.0, 1.0, L, dtype=F32)[:, None]
    bands = (HY_EMB - 1) // 2
    freqs = jnp.linspace(1e-4, bands - 1, bands, dtype=F32)[None, :]
    w = (2.0 * math.pi / L) * jnp.arange(L, dtype=F32)[:, None]
    emb = jnp.concatenate([t, jnp.cos(freqs * w), -jnp.sin(freqs * w)], axis=-1)
    emb = jnp.pad(emb, ((0, 0), (0, HY_EMB_PAD - HY_EMB)))
    w1p = jnp.pad(w1, ((0, HY_EMB_PAD - HY_EMB), (0, 0)))
    F = w1.shape[1]
    CC = w4.shape[1]
    deltas = jnp.abs(jnp.linspace(math.log(HY_DECAY_TARGET) / HY_SLOW_DECAY,
                                  math.log(HY_DECAY_TARGET) / HY_FAST_DECAY, C, dtype=F32))
    dl = jnp.tile(deltas, CC // C).reshape(1, CC)
    tl = min(L, 1024)
    tn = 1024
    full = lambda shape: pl.BlockSpec(shape, lambda j, i: (0, 0))
    h, s = pl.pallas_call(
        _hy_filter_kernel,
        grid=(CC // tn, L // tl),
        in_specs=[pl.BlockSpec((tl, HY_EMB_PAD), lambda j, i: (i, 0)),
                  full((HY_EMB_PAD, F)), full((1, F)), full((F, F)), full((1, F)), full((F, F)), full((1, F)),
                  pl.BlockSpec((F, tn), lambda j, i: (0, j)), pl.BlockSpec((1, tn), lambda j, i: (0, j))],
        out_specs=[pl.BlockSpec((tl, tn), lambda j, i: (i, j)), pl.BlockSpec((1, tn), lambda j, i: (0, j))],
        out_shape=[jax.ShapeDtypeStruct((L, CC), F32), jax.ShapeDtypeStruct((1, CC), F32)],
        compiler_params=_params("arbitrary", "arbitrary"),
        name="hy_filter",
    )(emb, w1p, b1.reshape(1, F), w2, b2.reshape(1, F), w3, b3.reshape(1, F), w4, dl)
    tot = s.reshape(HY_ORDER, 2, C).sum(axis=1, keepdims=True) + NORM_EPS
    inv = (1.0 / jnp.broadcast_to(tot, (HY_ORDER, 2, C))).reshape(1, CC)
    return h, inv


def dft_tables(L):
    n = 2 * L
    N1 = n // HY_N2
    A = N1 // 2
    NC = min(N1, -(-(A + 1) // 8) * 8)
    c = jnp.arange(NC, dtype=jnp.int32)
    a = jnp.arange(A, dtype=jnp.int32)
    ang1 = ((c[:, None] * a[None, :]) % N1).astype(F32) * (2.0 * math.pi / N1)
    fs = jnp.concatenate([jnp.cos(ang1), -jnp.sin(ang1)], axis=0)
    wr = jnp.where((c == 0) | (c == A), 1.0, jnp.where(c < A, 2.0, 0.0))[:, None]
    fi = jnp.concatenate([wr * jnp.cos(ang1), -wr * jnp.sin(ang1)], axis=0).T
    b = jnp.arange(HY_N2, dtype=jnp.int32)
    d = jnp.arange(HY_N2, dtype=jnp.int32)
    k2 = (b[None, None, :] * (c[:, None, None] + N1 * d[None, :, None])) % n
    ang2 = k2.astype(F32) * (2.0 * math.pi / n)
    cs, sn = jnp.cos(ang2), jnp.sin(ang2)
    gf = jnp.concatenate([jnp.concatenate([cs, sn], axis=2),
                          jnp.concatenate([-sn, cs], axis=2)], axis=1)
    return fs.astype(BF16), fi.astype(BF16), gf.astype(BF16), gf.transpose(0, 2, 1).astype(BF16)


DFT_SUB = 8


def _time_view(x):
    K, B, T, C = x.shape
    return x.reshape(K, B, T // HY_N2, HY_N2 // DFT_SUB, DFT_SUB, C)


LANES = 128


def _strided_rows(refs, first, count, stride):
    return jnp.concatenate([r[pl.ds(first, count, stride=stride), :] for r in refs], axis=1)


def _lane_tile_specs(block, index, n_tiles):
    def spec(t):
        def idx(*g):
            *lead, j = index(*g)
            return (*lead, j * n_tiles + t)
        return pl.BlockSpec(block + (LANES,), idx)
    return [spec(t) for t in range(n_tiles)]


def _dft_s1_kernel(f_ref, *refs):
    *x_refs, o_ref = refs
    A = x_refs[0].shape[0]
    flat = [r.reshape(A * DFT_SUB, LANES) for r in x_refs]
    NC = o_ref.shape[3]
    for bl in range(DFT_SUB):
        y = _dot(f_ref[...], _strided_rows(flat, bl, A, DFT_SUB))
        o_ref[0, 0, bl] = y[:NC]
        o_ref[0, 1, bl] = y[NC:]


def dft_s1(fs, src, k, tc):
    _, B, _, C = src.shape
    M, A = fs.shape
    nt = tc // LANES
    x_specs = _lane_tile_specs((None, None, A, None, DFT_SUB), lambda b, j, g: (k, b, 0, g, 0, j), nt)
    return pl.pallas_call(
        _dft_s1_kernel,
        grid=(B, C // tc, HY_N2 // DFT_SUB),
        in_specs=[pl.BlockSpec((M, A), lambda b, j, g: (0, 0))] + x_specs,
        out_specs=pl.BlockSpec((1, 2, DFT_SUB, M // 2, tc), lambda b, j, g: (b, 0, g, 0, j)),
        out_shape=jax.ShapeDtypeStruct((B, 2, HY_N2, M // 2, C), F32),
        compiler_params=_params("arbitrary", "arbitrary", "arbitrary"),
        name="dft_s1",
    )(fs, *([_time_view(src)] * nt))


def _pair_rows(refs):
    return [r.reshape(2 * r.shape[1] * DFT_SUB, LANES) for r in refs]


def _gather_pair(flat, n_rows, sub):
    return jnp.concatenate([_strided_rows(flat, sub, n_rows, DFT_SUB),
                            _strided_rows(flat, n_rows * DFT_SUB + sub, n_rows, DFT_SUB)], axis=0)


def _dft_mid_kernel(gf_ref, gi_ref, *refs):
    *y_refs, h_ref, o_ref = refs
    flat = _pair_rows(y_refs)
    group = 4
    for c0 in range(0, DFT_SUB, group):
        cls = range(c0, c0 + group)
        xs = [_dot(gf_ref[cl], _gather_pair(flat, HY_N2, cl)) for cl in cls]
        ps = []
        for cl, x in zip(cls, xs):
            xr, xi = x[:HY_N2], x[HY_N2:]
            hr, hi = h_ref[0, 0, 0, cl], h_ref[0, 1, 0, cl]
            ps.append(jnp.concatenate([xr * hr - xi * hi, xr * hi + xi * hr], axis=0).astype(BF16))
        qs = [_dot(gi_ref[cl], p) for cl, p in zip(cls, ps)]
        for cl, q in zip(cls, qs):
            o_ref[0, 0, cl] = q[:HY_N2]
            o_ref[0, 1, cl] = q[HY_N2:]


def dft_mid(gf, gi, y, hs, order, tc):
    B, _, N2, NC, C = y.shape
    nt = tc // LANES
    y6 = y.reshape(B, 2, N2, NC // DFT_SUB, DFT_SUB, C)
    hs6 = hs.reshape(HY_ORDER, 2, NC // DFT_SUB, DFT_SUB, N2, C)
    y_specs = _lane_tile_specs((None, 2, N2, None, DFT_SUB), lambda c, j, b: (b, 0, 0, c, 0, j), nt)
    return pl.pallas_call(
        _dft_mid_kernel,
        grid=(NC // DFT_SUB, C // tc, B),
        in_specs=[pl.BlockSpec((DFT_SUB, 2 * N2, 2 * N2), lambda c, j, b: (c, 0, 0)),
                  pl.BlockSpec((DFT_SUB, 2 * N2, 2 * N2), lambda c, j, b: (c, 0, 0))] + y_specs +
                 [pl.BlockSpec((1, 2, 1, DFT_SUB, N2, tc), lambda c, j, b: (order, 0, c, 0, 0, j))],
        out_specs=pl.BlockSpec((1, 2, DFT_SUB, N2, tc), lambda c, j, b: (b, 0, c, 0, j)),
        out_shape=jax.ShapeDtypeStruct((B, 2, NC, N2, C), F32),
        compiler_params=_params("arbitrary", "arbitrary", "arbitrary"),
        name="dft_mid",
    )(gf, gi, *([y6] * nt), hs6)


def _dft_filt_kernel(gf_ref, *refs):
    *y_refs, inv_ref, o_ref = refs
    nt = len(y_refs) // 2
    flat_f, flat_b = _pair_rows(y_refs[:nt]), _pair_rows(y_refs[nt:])
    inv = inv_ref[...]
    for cl in range(DFT_SUB):
        xf = _dot(gf_ref[cl], _gather_pair(flat_f, HY_N2, cl))
        xb = _dot(gf_ref[cl], _gather_pair(flat_b, HY_N2, cl))
        o_ref[0, 0, 0, cl] = (xf[:HY_N2] + xb[:HY_N2]) * inv
        o_ref[0, 1, 0, cl] = (xf[HY_N2:] - xb[HY_N2:]) * inv


def dft_filter_spectrum(gf, y, inv, C, tc):
    _, _, N2, NC, CC = y.shape
    nj = C // tc
    nt = tc // LANES
    y6 = y.reshape(1, 2, N2, NC // DFT_SUB, DFT_SUB, CC)
    block = (None, 2, N2, None, DFT_SUB)
    yf_specs = _lane_tile_specs(block, lambda c, o, j: (0, 0, 0, c, 0, (2 * o) * nj + j), nt)
    yb_specs = _lane_tile_specs(block, lambda c, o, j: (0, 0, 0, c, 0, (2 * o + 1) * nj + j), nt)
    out = pl.pallas_call(
        _dft_filt_kernel,
        grid=(NC // DFT_SUB, HY_ORDER, nj),
        in_specs=[pl.BlockSpec((DFT_SUB, 2 * N2, 2 * N2), lambda c, o, j: (c, 0, 0))] + yf_specs + yb_specs +
                 [pl.BlockSpec((1, tc), lambda c, o, j: (0, (2 * o) * nj + j))],
        out_specs=pl.BlockSpec((1, 2, 1, DFT_SUB, N2, tc), lambda c, o, j: (o, 0, c, 0, 0, j)),
        out_shape=jax.ShapeDtypeStruct((HY_ORDER, 2, NC // DFT_SUB, DFT_SUB, N2, C), F32),
        compiler_params=_params("arbitrary", "arbitrary", "arbitrary"),
        name="dft_filter_spectrum",
    )(gf, *([y6] * (2 * nt)), inv)
    return out.reshape(HY_ORDER, 2, NC, N2, C)


def _dft_i2_kernel(f_ref, *refs, inv_n):
    nt = (len(refs) - 2) // 3
    q_refs, z_refs, g_refs = refs[:nt], refs[nt:2 * nt], refs[2 * nt:3 * nt]
    bias_ref, o_ref = refs[3 * nt:]
    A = z_refs[0].shape[0]
    NC = q_refs[0].shape[1]
    q_flat = _pair_rows(q_refs)
    z_flat = [r.reshape(A * DFT_SUB, LANES) for r in z_refs]
    g_flat = [r.reshape(A * DFT_SUB, LANES) for r in g_refs]
    for bl in range(DFT_SUB):
        y = _dot(f_ref[...], _gather_pair(q_flat, NC, bl))
        z = _strided_rows(z_flat, bl, A, DFT_SUB)
        g = _strided_rows(g_flat, bl, A, DFT_SUB)
        o_ref[:, 0, bl, :] = g * (y * inv_n + bias_ref[...] * z)


def dft_i2(fi, q, zsrc, kz, gsrc, kg, bias, tc):
    A, M = fi.shape
    B, _, NC, _, C = q.shape
    ng = HY_N2 // DFT_SUB
    nt = tc // LANES
    q6 = q.reshape(B, 2, NC, ng, DFT_SUB, C)
    q_specs = _lane_tile_specs((None, 2, NC, None, DFT_SUB), lambda b, j, g: (b, 0, 0, g, 0, j), nt)
    tspecs = lambda k: _lane_tile_specs((None, None, A, None, DFT_SUB), lambda b, j, g: (k, b, 0, g, 0, j), nt)
    out = pl.pallas_call(
        functools.partial(_dft_i2_kernel, inv_n=1.0 / (2 * A * HY_N2)),
        grid=(B, C // tc, ng),
        in_specs=[pl.BlockSpec((A, M), lambda b, j, g: (0, 0))] + q_specs + tspecs(kz) + tspecs(kg) +
                 [pl.BlockSpec((1, tc), lambda b, j, g: (0, j))],
        out_specs=pl.BlockSpec((None, A, 1, DFT_SUB, tc), lambda b, j, g: (b, 0, g, 0, j)),
        out_shape=jax.ShapeDtypeStruct((B, A, ng, DFT_SUB, C), F32),
        compiler_params=_params("arbitrary", "arbitrary", "arbitrary"),
        name="dft_i2",
    )(fi, *([q6] * nt), *([_time_view(zsrc)] * nt), *([_time_view(gsrc)] * nt), bias)
    return out.reshape(B, A * HY_N2, C)


def _hy_ctx_kernel(fd_ref, fdi_ref, pc_ref, h_ref, bias_ref, o_ref, *, n):
    hp = lax.Precision.HIGHEST
    dot = lambda a, b: jnp.dot(a, b, preferred_element_type=F32, precision=hp)
    fd = fd_ref[...]
    z = pc_ref[0, 0]
    for o in range(HY_ORDER):
        hf = dot(fd, h_ref[2 * o])
        hb = dot(fd, h_ref[2 * o + 1])
        hr = hf[:n] + hb[:n]
        hi = hf[n:] - hb[n:]
        x = dot(fd, z)
        xr, xi = x[:n], x[n:]
        p = jnp.concatenate([xr * hr - xi * hi, xr * hi + xi * hr], axis=0)
        y = dot(fdi_ref[...], p) * (1.0 / n)
        z = pc_ref[o + 1, 0] * (y + bias_ref[o:o + 1] * z)
    o_ref[0] = z


def hy_ctx(pc, filt_ctx, hy_bias, S):
    _, B, T, C = pc.shape
    Lc = T - S
    n = 2 * Lc
    f = jnp.arange(n, dtype=jnp.int32)
    t = jnp.arange(Lc, dtype=jnp.int32)
    ang = ((f[:, None] * t[None, :]) % n).astype(F32) * (2.0 * math.pi / n)
    fd = jnp.concatenate([jnp.cos(ang), -jnp.sin(ang)], axis=0)
    h4 = filt_ctx.reshape(Lc, 2 * HY_ORDER, C).transpose(1, 0, 2)
    tc = 256
    return pl.pallas_call(
        functools.partial(_hy_ctx_kernel, n=n),
        grid=(B, C // tc),
        in_specs=[pl.BlockSpec((2 * n, Lc), lambda b, j: (0, 0)),
                  pl.BlockSpec((Lc, 2 * n), lambda b, j: (0, 0)),
                  pl.BlockSpec((HY_ORDER + 1, 1, Lc, tc), lambda b, j: (0, b, S // Lc, j)),
                  pl.BlockSpec((2 * HY_ORDER, Lc, tc), lambda b, j: (0, 0, j)),
                  pl.BlockSpec((HY_ORDER, tc), lambda b, j: (0, j))],
        out_specs=pl.BlockSpec((1, Lc, tc), lambda b, j: (b, 0, j)),
        out_shape=jax.ShapeDtypeStruct((B, Lc, C), F32),
        compiler_params=_params("arbitrary", "arbitrary"),
        name="hy_ctx",
    )(fd, fd.T, pc, h4, hy_bias)


def hyena_latent(pc, filt, hy_bias, S, tabs):
    fs, fi, gf, gi = tabs
    _, B, T, C = pc.shape
    h_raw, inv = filt
    hy = dft_s1(fs, h_raw.reshape(1, 1, S, 2 * HY_ORDER * C), 0, 512)
    hs = dft_filter_spectrum(gf, hy, inv, C, 512)
    zsrc, kz = pc, 0
    for o in range(HY_ORDER):
        y = dft_s1(fs, zsrc, kz, 512)
        q = dft_mid(gf, gi, y, hs, o, 512)
        z = dft_i2(fi, q, zsrc, kz, pc, o + 1, hy_bias[o].reshape(1, C), 512)
        zsrc, kz = z.reshape(1, B, S, C), 0
    return z


def _flat(a):
    return a.reshape(a.shape[0] * a.shape[1], a.shape[2])


def _proj(h, w, out_dtype, tn):
    B, T, D = h.shape
    M = B * T
    tm = 1024 if M % 1024 == 0 else ROW_TILE
    return mm(_flat(h), w, out_dtype, tm, tn).reshape(B, T, w.shape[1])


def even_layer(h, xs, modtab, gain_ffn, S, w_in, w_out, rpb, gk_up, gk_bias, gnorm):
    B, T, D = xs.shape
    na_w = w_in.shape[1] - 2 * GLA_HEADS * GLA_DK - 2 * (D // 2) - 2 * GLA_LOWRANK
    na_w //= 3
    qk_w = GLA_HEADS * GLA_DK
    v_w = D // 2
    dv = v_w // GLA_HEADS
    o = 3 * na_w
    w_na = w_in[:, :o].astype(BF16)
    wq, wk = w_in[:, o:o + qk_w], w_in[:, o + qk_w:o + 2 * qk_w]
    wv = w_in[:, o + 2 * qk_w:o + 2 * qk_w + v_w]
    wg = w_in[:, o + 2 * qk_w + v_w:o + 2 * qk_w + 2 * v_w]
    wlr = w_in[:, o + 2 * qk_w + 2 * v_w:]
    per_head = lambda w, width: w.reshape(D, GLA_HEADS, width)
    w_gl = jnp.concatenate([per_head(wq, GLA_DK), per_head(_rot_partner_cols(wq), GLA_DK),
                            per_head(wk, GLA_DK), per_head(_rot_partner_cols(wk), GLA_DK),
                            per_head(wv, dv)], axis=-1).reshape(D, GLA_HEADS * GLA_HEAD_COLS).astype(BF16)
    w_g = wg.astype(BF16)
    w_lr = jnp.pad(wlr, ((0, 0), (0, 128 - 2 * GLA_LOWRANK))).astype(BF16)

    qkv = _proj(h, w_na, BF16, 1024)
    gl = _proj(h, w_gl, F32, 768)
    g = _proj(h, w_g, F32, 1024)
    lr = _proj(h, w_lr, F32, 128)

    ya = na_attention(qkv, na_bias_table(rpb, S), S)
    cos_tab, sin_tab = rope_tables(S, T)
    up_tied = _tie_pairs(gk_up.astype(F32)).transpose(2, 0, 1, 3)
    up_tied = jnp.stack([jnp.pad(up_tied[:, 0], ((0, 0), (0, 128 - GLA_LOWRANK), (0, 0))),
                         jnp.pad(up_tied[:, 1], ((0, 0), (GLA_LOWRANK, 128 - 2 * GLA_LOWRANK), (0, 0)))],
                        axis=1)
    bias_tied = _tie_pairs(gk_bias.astype(F32)).transpose(1, 0, 2)[:, :, None, :]
    o_f, o_b = gla_scan(gl, lr, cos_tab, sin_tab, up_tied, bias_tied, S)
    w_out_b = w_out.astype(BF16)
    return mm_res([ya], [w_out_b[:na_w], w_out_b[na_w:]], xs, modtab, gain_ffn, S, gla=(o_f, o_b, g, gnorm))


def odd_layer(h, xs, modtab, gain_ffn, S, ctx_live, w_in, w_out, hy_short, hyp, hy_bias, sc_conv, tabs):
    B, T, D = xs.shape
    C = D // 2
    n_hy = (HY_ORDER + 1) * C
    w = w_in.astype(BF16)
    pc = hy_pre(_proj(h, w[:, :n_hy], F32, 1024), hy_short, S)
    y_sc = sc_mix(_proj(h, w[:, n_hy:], F32, 1024), sc_conv, S)
    z_lat = hyena_latent(pc, hyena_filters(S, C, *hyp), hy_bias, S, tabs)
    if ctx_live:
        h_ctx, inv_ctx = hyena_filters(T - S, C, *hyp)
        z_ctx = hy_ctx(pc, h_ctx * inv_ctx, hy_bias, S)
    else:
        z_ctx = jnp.zeros((B, T - S, C), F32)
    z = jnp.concatenate([z_lat, z_ctx], axis=1)
    w_out_b = w_out.astype(BF16)
    return mm_res([z, y_sc], [w_out_b[:C], w_out_b[C:]], xs, modtab, gain_ffn, S)


def kernel(x, c, ctx, c_ctx, w_mod, b_mod, norm_mix, norm_ffn, norm_final, w_in_even, w_out_even, na_rpb,
           gla_gk_up, gla_gk_bias, gla_norm, w_in_odd, w_out_odd, hy_short, hy_w1, hy_b1, hy_w2, hy_b2,
           hy_w3, hy_b3, hy_w4, hy_bias, sc_conv, moe_w_group, moe_b_group, moe_w_expert, moe_b_expert,
           moe_w1, moe_w3, moe_w2):
    B, S, D = x.shape
    Lc = ctx.shape[1]
    T = S + Lc
    depth = w_mod.shape[0]
    last_even = 2 * ((depth - 1) // 2)
    xs = jnp.concatenate([x, ctx], axis=1)
    tabs = dft_tables(S)

    cvec = jnp.concatenate([c, c_ctx[None, :], jnp.zeros((8 - B - 1, D), F32)], axis=0)
    cvec = cvec * jax.nn.sigmoid(cvec)
    modtabs = []
    for l in range(depth):
        mod = mm(cvec, w_mod, F32, 8, 1024, layer=l)[:B + 1] + b_mod[l]
        mod = mod.reshape(B + 1, 6, D)
        cx = jnp.broadcast_to(mod[B][None], (B, 6, D))
        modtabs.append(jnp.pad(jnp.stack([mod[:B], cx], axis=1), ((0, 0), (0, 0), (0, 2), (0, 0))))

    h = normmod(xs, norm_mix[0], modtabs[0], MOD_MIX, S, BF16)
    for l in range(depth):
        modtab = modtabs[l]
        if l % 2 == 0:
            e = l // 2
            xs, tok = even_layer(h, xs, modtab, norm_ffn[l], S, w_in_even[e], w_out_even[e], na_rpb[e],
                                 gla_gk_up[e], gla_gk_bias[e], gla_norm[e])
        else:
            o = l // 2
            hyp = (hy_w1[o], hy_b1[o], hy_w2[o], hy_b2[o], hy_w3[o], hy_b3[o], hy_w4[o])
            xs, tok = odd_layer(h, xs, modtab, norm_ffn[l], S, l < last_even, w_in_odd[o], w_out_odd[o],
                                hy_short[o], hyp, hy_bias[o], sc_conv[o], tabs)

        w_router = jnp.pad(jnp.concatenate([moe_w_group[l], moe_w_expert[l]], axis=1),
                           ((0, 0), (0, 128 - N_GROUPS - N_EXPERTS))).astype(BF16)
        b_router = jnp.pad(jnp.concatenate([moe_b_group[l], moe_b_expert[l]]),
                           (0, 128 - N_GROUPS - N_EXPERTS)).reshape(1, 128)
        pair = hier_moe(_flat(tok), w_router, b_router, l, moe_w1, moe_w3, moe_w2)
        if l + 1 < depth:
            xs, h = moe_combine(xs, pair, modtab, norm_mix[l + 1], modtabs[l + 1], S, False)
        else:
            (out,) = moe_combine(xs, pair, modtab, norm_final, modtab, S, True)
    return out
```

```python
import functools
import math

import jax
import jax.numpy as jnp
from jax import lax
from jax.experimental import pallas as pl
from jax.experimental.pallas import tpu as pltpu

F32 = jnp.float32
BF16 = jnp.bfloat16

NORM_EPS = 1e-6
NEG_INF = -1e30
ROPE_THETA = 10000.0
GRID_W = 64
HEAD_DIM = 128
NA_WIN_ROWS = 8
NA_WIN_COLS = 16
GLA_HEADS = 4
GLA_DK = 128
GLA_LOWRANK = 16
GLA_GATE_NORMALIZER = 16.0
GLA_CHUNK = 64
HY_ORDER = 2
HY_EMB = 33
HY_SIN_FREQ = 1.0
HY_DECAY_TARGET = 1e-2
HY_FAST_DECAY = 0.3
HY_SLOW_DECAY = 1.5
CONV_W = 3
N_GROUPS = 4
EXPERTS_PER_GROUP = 8
N_EXPERTS = N_GROUPS * EXPERTS_PER_GROUP
TOP_K_IN_GROUP = 2

ROW_TILE = 256
NA_GROUP_ROWS = 4
NA_SPAN_ROWS = NA_GROUP_ROWS + NA_WIN_ROWS
MOE_TILE = 512
VMEM_LIMIT = 56 * 1024 * 1024


def _params(*sem):
    return pltpu.CompilerParams(dimension_semantics=sem, vmem_limit_bytes=VMEM_LIMIT)


def _dot(a, b):
    return jnp.dot(a.astype(BF16), b.astype(BF16), preferred_element_type=F32)


def _dot_nt(a, b):
    return lax.dot_general(a.astype(BF16), b.astype(BF16), (((1,), (1,)), ((), ())),
                           preferred_element_type=F32)


def _mm_kernel(a_ref, w_ref, o_ref):
    o_ref[...] = _dot(a_ref[...], w_ref[...]).astype(o_ref.dtype)


def mm(a, w, out_dtype, tm, tn, layer=None):
    M, K = a.shape
    N = w.shape[-1]
    assert M % tm == 0 and N % tn == 0, (M, N, tm, tn)
    if layer is None:
        w_spec = pl.BlockSpec((K, tn), lambda j, i: (0, j))
    else:
        w_spec = pl.BlockSpec((None, K, tn), lambda j, i: (layer, 0, j))
    return pl.pallas_call(
        _mm_kernel,
        grid=(N // tn, M // tm),
        in_specs=[pl.BlockSpec((tm, K), lambda j, i: (i, 0)), w_spec],
        out_specs=pl.BlockSpec((tm, tn), lambda j, i: (i, j)),
        out_shape=jax.ShapeDtypeStruct((M, N), out_dtype),
        compiler_params=_params("arbitrary", "arbitrary"),
        name="mm",
    )(a, w)


MOD_MIX = (0, 1, 2)
MOD_FFN = (3, 4, 5)


def _norm_modulate(x, gain, mod, rows):
    y = x * lax.rsqrt(jnp.mean(x * x, axis=-1, keepdims=True) + NORM_EPS) * gain
    return y * (1.0 + mod[rows[1]:rows[1] + 1]) + mod[rows[0]:rows[0] + 1]


def _embed_kernel(x_ref, c_ref, g_ref, mod_ref, xs_ref, h_ref, *, n_lat):
    v = jnp.where(pl.program_id(1) >= n_lat, c_ref[0], x_ref[0])
    xs_ref[0] = v
    h_ref[0] = _norm_modulate(v, g_ref[...], mod_ref[0, 0], MOD_MIX).astype(h_ref.dtype)


def embed(x, ctx, gain, modtab):
    B, S, D = x.shape
    Lc = ctx.shape[1]
    assert Lc == ROW_TILE and S % ROW_TILE == 0
    n_lat = S // ROW_TILE
    row = pl.BlockSpec((1, ROW_TILE, D), lambda b, i: (b, i, 0))
    return pl.pallas_call(
        functools.partial(_embed_kernel, n_lat=n_lat),
        grid=(B, n_lat + 1),
        in_specs=[pl.BlockSpec((1, ROW_TILE, D), lambda b, i: (b, jnp.minimum(i, n_lat - 1), 0)),
                  pl.BlockSpec((1, ROW_TILE, D), lambda b, i: (b, 0, 0)),
                  pl.BlockSpec((1, D), lambda b, i: (0, 0)),
                  pl.BlockSpec((1, 1, 8, D), lambda b, i: (b, jnp.where(i >= n_lat, 1, 0), 0, 0))],
        out_specs=[row, row],
        out_shape=[jax.ShapeDtypeStruct((B, S + Lc, D), F32), jax.ShapeDtypeStruct((B, S + Lc, D), BF16)],
        compiler_params=_params("arbitrary", "arbitrary"),
        name="embed",
    )(x, ctx, gain.reshape(1, D), modtab)


def _gla_output(of_ref, ob_ref, g_ref, gn_ref):
    dv = gn_ref.shape[-1]
    gn = gn_ref[...]
    parts = []
    for h in range(of_ref.shape[-1] // dv):
        sl = slice(h * dv, (h + 1) * dv)
        o = of_ref[0, :, sl] + ob_ref[0, :, sl]
        o = o * lax.rsqrt(jnp.mean(o * o, axis=-1, keepdims=True) + NORM_EPS) * gn
        g = g_ref[0, :, sl]
        parts.append((o * (g * jax.nn.sigmoid(g))).astype(BF16))
    return jnp.concatenate(parts, axis=-1)


def _mm_res_kernel(*refs, n_a, n_gla):
    a_refs = refs[:n_a]
    gla_refs = refs[n_a:n_a + n_gla]
    n_w = n_a + (1 if n_gla else 0)
    w_refs = refs[n_a + n_gla:n_a + n_gla + n_w]
    res_ref, mod_ref, gain_ref, o_ref, t_ref = refs[n_a + n_gla + n_w:]
    a_vals = [a_ref[0] for a_ref in a_refs] + ([_gla_output(*gla_refs)] if n_gla else [])
    acc = _dot(a_vals[0], w_refs[0][...])
    for a, w_ref in zip(a_vals[1:], w_refs[1:]):
        acc = acc + _dot(a, w_ref[...])
    mod = mod_ref[0, 0]
    x = res_ref[0] + mod[MOD_MIX[2]:MOD_MIX[2] + 1] * acc
    o_ref[0] = x
    t_ref[0] = _norm_modulate(x, gain_ref[...], mod, MOD_FFN).astype(t_ref.dtype)


def mm_res(a_list, w_list, res, modtab, gain_ffn, S, gla=None):
    B, T, D = res.shape
    n_lat = S // ROW_TILE
    n_a = len(a_list)
    row = pl.BlockSpec((1, ROW_TILE, D), lambda b, i: (b, i, 0))
    arow = lambda a: pl.BlockSpec((1, ROW_TILE, a.shape[-1]), lambda b, i: (b, i, 0))
    in_specs = [arow(a) for a in a_list]
    gla_args = []
    if gla is not None:
        o_f, o_b, g, gnorm = gla
        gla_args = [o_f, o_b, g, gnorm.reshape(1, -1)]
        in_specs += [arow(o_f), arow(o_b), arow(g), pl.BlockSpec((1, gnorm.shape[0]), lambda b, i: (0, 0))]
    in_specs += [pl.BlockSpec(w.shape, lambda b, i: (0, 0)) for w in w_list]
    in_specs += [row, pl.BlockSpec((1, 1, 8, D), lambda b, i: (b, jnp.where(i >= n_lat, 1, 0), 0, 0)),
                 pl.BlockSpec((1, D), lambda b, i: (0, 0))]
    return pl.pallas_call(
        functools.partial(_mm_res_kernel, n_a=n_a, n_gla=len(gla_args)),
        grid=(B, T // ROW_TILE),
        in_specs=in_specs,
        out_specs=[row, row],
        out_shape=[jax.ShapeDtypeStruct((B, T, D), F32), jax.ShapeDtypeStruct((B, T, D), BF16)],
        compiler_params=_params("arbitrary", "arbitrary"),
        name="mm_res",
    )(*a_list, *gla_args, *w_list, res, modtab, gain_ffn.reshape(1, D))


def _moe_combine_kernel(x_ref, p_ref, mod_ref, gain_ref, modn_ref, *out_refs, final):
    f = p_ref[0, 0].astype(F32) + p_ref[1, 0].astype(F32)
    x = x_ref[0] + mod_ref[0, 0][MOD_FFN[2]:MOD_FFN[2] + 1] * f
    if final:
        out_refs[0][0] = x * lax.rsqrt(jnp.mean(x * x, axis=-1, keepdims=True) + NORM_EPS) * gain_ref[...]
    else:
        out_refs[0][0] = x
        out_refs[1][0] = _norm_modulate(x, gain_ref[...], modn_ref[0, 0], MOD_MIX).astype(out_refs[1].dtype)


def moe_combine(xs, pair, modtab, gain_next, modtab_next, S, final):
    B, T, D = xs.shape
    n_lat = S // ROW_TILE
    rows = S if final else T
    row = pl.BlockSpec((1, ROW_TILE, D), lambda b, i: (b, i, 0))
    mspec = pl.BlockSpec((1, 1, 8, D), lambda b, i: (b, jnp.where(i >= n_lat, 1, 0), 0, 0))
    if final:
        out_specs, out_shape = [row], [jax.ShapeDtypeStruct((B, S, D), F32)]
    else:
        out_specs = [row, row]
        out_shape = [jax.ShapeDtypeStruct((B, T, D), F32), jax.ShapeDtypeStruct((B, T, D), BF16)]
    return pl.pallas_call(
        functools.partial(_moe_combine_kernel, final=final),
        grid=(B, rows // ROW_TILE),
        in_specs=[row, pl.BlockSpec((2, 1, ROW_TILE, D), lambda b, i: (0, b, i, 0)), mspec,
                  pl.BlockSpec((1, D), lambda b, i: (0, 0)), mspec],
        out_specs=out_specs,
        out_shape=out_shape,
        compiler_params=_params("arbitrary", "arbitrary"),
        name="moe_combine",
    )(xs, pair.reshape(2, B, T, D), modtab, gain_next.reshape(1, D), modtab_next)


NA_HEADS_PER_STEP = 4


def _na_kernel(q_ref, k_ref, v_ref, bias_ref, o_ref, *, S, Lc, n_lat):
    i = pl.program_id(2)
    rows = S // GRID_W
    scale = HEAD_DIM ** -0.5
    heads = [slice(h * HEAD_DIM, (h + 1) * HEAD_DIM) for h in range(NA_HEADS_PER_STEP)]
    q = [q_ref[0, :, hs] for hs in heads]
    vc = [v_ref[0, pl.ds(S, Lc), hs] for hs in heads]
    s_ctx = [_dot_nt(q[h], k_ref[0, pl.ds(S, Lc), hs]) * scale for h, hs in enumerate(heads)]

    @pl.when(i < n_lat)
    def _():
        ks = jnp.clip(NA_GROUP_ROWS * i - NA_WIN_ROWS // 2, 0, rows - NA_SPAN_ROWS)
        start = pl.multiple_of(ks * GRID_W, GRID_W)
        span = pl.ds(start, NA_SPAN_ROWS * GRID_W)
        s_loc = [_dot_nt(q[h], k_ref[0, span, hs]) * scale + bias_ref[h, 0] for h, hs in enumerate(heads)]
        m = [jnp.maximum(jnp.max(s_loc[h], axis=-1, keepdims=True), jnp.max(s_ctx[h], axis=-1, keepdims=True))
             for h in range(NA_HEADS_PER_STEP)]
        p_loc = [jnp.exp(s_loc[h] - m[h]) for h in range(NA_HEADS_PER_STEP)]
        p_ctx = [jnp.exp(s_ctx[h] - m[h]) for h in range(NA_HEADS_PER_STEP)]
        for h, hs in enumerate(heads):
            den = jnp.sum(p_loc[h], axis=-1, keepdims=True) + jnp.sum(p_ctx[h], axis=-1, keepdims=True)
            o = _dot(p_loc[h], v_ref[0, span, hs]) + _dot(p_ctx[h], vc[h])
            o_ref[0, :, hs] = (o / den).astype(o_ref.dtype)

    @pl.when(i >= n_lat)
    def _():
        for h, hs in enumerate(heads):
            m = jnp.max(s_ctx[h], axis=-1, keepdims=True)
            p = jnp.exp(s_ctx[h] - m)
            o = _dot(p, vc[h])
            o_ref[0, :, hs] = (o / jnp.sum(p, axis=-1, keepdims=True)).astype(o_ref.dtype)


def na_bias_table(rpb, S):
    rows = S // GRID_W
    n_lat = S // ROW_TILE
    kc = NA_WIN_COLS
    cols = jnp.arange(GRID_W)
    col_start = jnp.clip(cols - kc // 2, 0, GRID_W - kc)
    col_ok = (cols[None, :] >= col_start[:, None]) & (cols[None, :] < col_start[:, None] + kc)
    coff = jnp.clip(cols[None, :] - cols[:, None], -(kc - 1), kc - 1) + kc - 1
    cb = jnp.take(rpb.astype(F32), coff, axis=-1)
    cb = jnp.where(col_ok[None, None], cb, NEG_INF)
    g = jnp.array([0, min(1, n_lat - 1), n_lat - 1])
    ks = jnp.clip(NA_GROUP_ROWS * g - NA_WIN_ROWS // 2, 0, rows - NA_SPAN_ROWS)
    r = NA_GROUP_ROWS * g[:, None] + jnp.arange(NA_GROUP_ROWS)[None, :]
    k = ks[:, None] + jnp.arange(NA_SPAN_ROWS)[None, :]
    rs = jnp.clip(r - NA_WIN_ROWS // 2, 0, rows - NA_WIN_ROWS)
    valid = (k[:, None, :] >= rs[:, :, None]) & (k[:, None, :] < rs[:, :, None] + NA_WIN_ROWS)
    delta = jnp.clip(k[:, None, :] - r[:, :, None] + NA_WIN_ROWS - 1, 0, 2 * NA_WIN_ROWS - 2)
    t = cb[:, delta]
    t = jnp.where(valid[None, :, :, :, None, None], t, NEG_INF)
    t = t.transpose(0, 1, 2, 4, 3, 5)
    H = rpb.shape[0]
    return t.reshape(H, 3, NA_GROUP_ROWS * GRID_W, NA_SPAN_ROWS * GRID_W)


def na_attention(qkv, bias_tab, S):
    B, T, W3 = qkv.shape
    H = W3 // (3 * HEAD_DIM)
    Lc = T - S
    n_lat = S // ROW_TILE
    assert Lc == ROW_TILE and NA_GROUP_ROWS * GRID_W == ROW_TILE
    span = NA_SPAN_ROWS * GRID_W

    hps = NA_HEADS_PER_STEP
    hw = hps * HEAD_DIM
    ng = H // hps

    def bias_idx(b, h, i):
        return (h, jnp.where(i == 0, 0, jnp.where(i == n_lat - 1, 2, 1)), 0, 0)

    return pl.pallas_call(
        functools.partial(_na_kernel, S=S, Lc=Lc, n_lat=n_lat),
        grid=(B, ng, n_lat + 1),
        in_specs=[pl.BlockSpec((1, ROW_TILE, hw), lambda b, h, i: (b, i, h)),
                  pl.BlockSpec((1, T, hw), lambda b, h, i: (b, 0, ng + h)),
                  pl.BlockSpec((1, T, hw), lambda b, h, i: (b, 0, 2 * ng + h)),
                  pl.BlockSpec((hps, 1, ROW_TILE, span), bias_idx)],
        out_specs=pl.BlockSpec((1, ROW_TILE, hw), lambda b, h, i: (b, i, h)),
        out_shape=jax.ShapeDtypeStruct((B, T, H * HEAD_DIM), BF16),
        compiler_params=_params("arbitrary", "arbitrary", "arbitrary"),
        name="na_attention",
    )(qkv, qkv, qkv, bias_tab)


GLA_HEAD_COLS = 4 * GLA_DK + 2 * GLA_DK


GLA_HEADS_PER_STEP = 4


def _gla_prep(x, lr, cos, sin, up, gb, reverse):
    R = x.shape[0]
    C = GLA_CHUNK
    dk = GLA_DK
    q = (x[:, 0:dk] * cos + x[:, dk:2 * dk] * sin) * (dk ** -0.5)
    k = x[:, 2 * dk:3 * dk] * cos + x[:, 3 * dk:4 * dk] * sin
    v = x[:, 4 * dk:6 * dk]
    z = _dot(lr, up) + gb
    la = (jnp.minimum(z, 0.0) - jnp.log(1.0 + jnp.exp(-jnp.abs(z)))) / GLA_GATE_NORMALIZER
    ri = lax.broadcasted_iota(jnp.int32, (R, R), 0)
    ci = lax.broadcasted_iota(jnp.int32, (R, R), 1)
    same = (ri // C) == (ci // C)
    cum = jnp.logical_and(same, (ci >= ri) if reverse else (ci <= ri)).astype(BF16)
    la_hi = la.astype(BF16)
    la_lo = la - la_hi.astype(F32)
    b = _dot(cum, la_hi) + _dot(cum, la_lo)
    return q, k, v, b


def _gla_kernel(xf_ref, xb_ref, lrf_ref, lrb_ref, cf_ref, sf_ref, cb_ref, sb_ref, up_ref, gb_ref,
                of_ref, ob_ref, state_ref):
    @pl.when(pl.program_id(2) == 0)
    def _():
        state_ref[...] = jnp.zeros_like(state_ref)

    C = GLA_CHUNK
    dk = GLA_DK
    dv = 2 * dk
    n_chunks = ROW_TILE // C
    rc = lax.broadcasted_iota(jnp.int32, (C, C), 0)
    cc = lax.broadcasted_iota(jnp.int32, (C, C), 1)
    chains = []
    for h in range(GLA_HEADS_PER_STEP):
        cols = slice(h * GLA_HEAD_COLS, (h + 1) * GLA_HEAD_COLS)
        for d, (x_ref, lr_ref, c_ref, s_ref, o_ref) in enumerate(
                ((xf_ref, lrf_ref, cf_ref, sf_ref, of_ref), (xb_ref, lrb_ref, cb_ref, sb_ref, ob_ref))):
            q, k, v, b = _gla_prep(x_ref[0, :, cols], lr_ref[0], c_ref[...], s_ref[...], up_ref[h, d], gb_ref[h, d],
                                   d == 1)
            chains.append(dict(q=q, k=k, v=v, b=b, d=d, h=h, o_ref=o_ref, state=state_ref[d, h]))
    for step in range(n_chunks):
        for ch in chains:
            rev = ch["d"] == 1
            c = n_chunks - 1 - step if rev else step
            sl = slice(c * C, (c + 1) * C)
            mid = C // 2 if rev else C // 2 - 1
            last = 0 if rev else C - 1
            bc, qc, kc = ch["b"][sl], ch["q"][sl], ch["k"][sl]
            b_mid = bc[mid:mid + 1]
            b_last = bc[last:last + 1]
            att = _dot_nt(qc * jnp.exp(bc - b_mid), kc * jnp.exp(b_mid - bc))
            ch.update(sl=sl, att=jnp.where((cc >= rc) if rev else (cc <= rc), att, 0.0),
                      qb=qc * jnp.exp(bc), kd_t=(kc * jnp.exp(b_last - bc)).T,
                      dec=jnp.exp(jnp.broadcast_to(b_last, (dk, dk)).T[:, :1]))
        for ch in chains:
            vc = ch["v"][ch["sl"]]
            o = _dot(ch["att"], vc) + _dot(ch["qb"], ch["state"])
            ch["o_ref"][0, ch["sl"], ch["h"] * dv:(ch["h"] + 1) * dv] = o
            ch["state"] = ch["dec"] * ch["state"] + _dot(ch["kd_t"], vc)
    for ch in chains:
        state_ref[ch["d"], ch["h"]] = ch["state"]


def gla_scan(gl, lr, cos_tab, sin_tab, up_tied, bias_tied, S):
    B, T, W = gl.shape
    H = W // GLA_HEAD_COLS
    n_blk = T // ROW_TILE
    dv = 2 * GLA_DK
    hps = GLA_HEADS_PER_STEP
    blk_f = lambda i: jnp.where(i == 0, n_blk - 1, i - 1)
    blk_b = lambda i: jnp.where(i == 0, n_blk - 1, n_blk - 1 - i)
    xspec = lambda blk: pl.BlockSpec((1, ROW_TILE, hps * GLA_HEAD_COLS), lambda b, h, i: (b, blk(i), h))
    lspec = lambda blk: pl.BlockSpec((1, ROW_TILE, 128), lambda b, h, i: (b, blk(i), 0))
    tspec = lambda blk: pl.BlockSpec((ROW_TILE, GLA_DK), lambda b, h, i: (blk(i), 0))
    ospec = lambda blk: pl.BlockSpec((1, ROW_TILE, hps * dv), lambda b, h, i: (b, blk(i), h))
    out = jax.ShapeDtypeStruct((B, T, H * dv), F32)
    return pl.pallas_call(
        _gla_kernel,
        grid=(B, H // hps, n_blk),
        in_specs=[xspec(blk_f), xspec(blk_b), lspec(blk_f), lspec(blk_b),
                  tspec(blk_f), tspec(blk_f), tspec(blk_b), tspec(blk_b),
                  pl.BlockSpec((hps, 2, 128, GLA_DK), lambda b, h, i: (h, 0, 0, 0)),
                  pl.BlockSpec((hps, 2, 1, GLA_DK), lambda b, h, i: (h, 0, 0, 0))],
        out_specs=[ospec(blk_f), ospec(blk_b)],
        out_shape=[out, out],
        scratch_shapes=[pltpu.VMEM((2, hps, GLA_DK, dv), F32)],
        compiler_params=_params("arbitrary", "arbitrary", "arbitrary"),
        name="gla_scan",
    )(gl, gl, lr, lr, cos_tab, sin_tab, cos_tab, sin_tab, up_tied, bias_tied)


def rope_tables(S, T):
    half = GLA_DK // 2
    nf = half // 2
    pos = jnp.arange(S)
    inv_freq = ROPE_THETA ** (-jnp.arange(nf, dtype=F32) / nf)

    def part(p):
        ang = p.astype(F32)[:, None] * inv_freq[None, :]
        c, s = jnp.cos(ang), jnp.sin(ang)
        return jnp.concatenate([c, c], axis=-1), jnp.concatenate([-s, s], axis=-1)

    cr, sr = part(pos // GRID_W)
    cc, sc = part(pos % GRID_W)
    cos = jnp.concatenate([cr, cc], axis=-1)
    sin = jnp.concatenate([sr, sc], axis=-1)
    cos = jnp.concatenate([cos, jnp.ones((T - S, GLA_DK), F32)], axis=0)
    sin = jnp.concatenate([sin, jnp.zeros((T - S, GLA_DK), F32)], axis=0)
    return cos, sin


def _rot_partner_cols(w):
    K = w.shape[0]
    w4 = w.reshape(K, GLA_HEADS, 4, GLA_DK // 4)
    return w4[:, :, jnp.array([1, 0, 3, 2])].reshape(K, GLA_HEADS * GLA_DK)


def _tie_pairs(u):
    lead = u.shape[:-1]
    u = u.reshape(lead + (GLA_HEADS, 2, 1, GLA_DK // 4))
    u = jnp.broadcast_to(u, lead + (GLA_HEADS, 2, 2, GLA_DK // 4))
    return u.reshape(lead + (GLA_HEADS, GLA_DK))


def _moe_kernel(be_ref, bv_ref, x_ref, w1_ref, w3_ref, w2_ref, g_ref, o_ref, w1b, w3b, w2b):
    i = pl.program_id(0)
    changed = jnp.logical_or(i == 0, be_ref[i] != be_ref[jnp.maximum(i - 1, 0)])

    @pl.when(changed)
    def _():
        w1b[...] = w1_ref[0, 0].astype(BF16)
        w3b[...] = w3_ref[0, 0].astype(BF16)
        w2b[...] = w2_ref[0, 0].astype(BF16)

    @pl.when(bv_ref[i] != 0)
    def _():
        x = x_ref[...]
        h1 = jnp.dot(x, w1b[...], preferred_element_type=F32)
        h3 = jnp.dot(x, w3b[...], preferred_element_type=F32)
        a = (h1 * jax.nn.sigmoid(h1)) * h3
        y = jnp.dot(a.astype(BF16), w2b[...], preferred_element_type=F32)
        g = g_ref[...]
        o_ref[...] = jnp.where(g != 0.0, y * g, 0.0).astype(o_ref.dtype)

    @pl.when(bv_ref[i] == 0)
    def _():
        o_ref[...] = jnp.zeros_like(o_ref)


def moe_experts(xs, layer, w1, w3, w2, slot_gate, block_expert, block_valid):
    cap, D = xs.shape
    De = w1.shape[-1]
    n_blocks = cap // MOE_TILE
    return pl.pallas_call(
        _moe_kernel,
        grid_spec=pltpu.PrefetchScalarGridSpec(
            num_scalar_prefetch=2,
            grid=(n_blocks,),
            in_specs=[pl.BlockSpec((MOE_TILE, D), lambda i, be, bv: (i, 0)),
                      pl.BlockSpec((1, 1, D, De), lambda i, be, bv: (layer, be[i], 0, 0)),
                      pl.BlockSpec((1, 1, D, De), lambda i, be, bv: (layer, be[i], 0, 0)),
                      pl.BlockSpec((1, 1, De, D), lambda i, be, bv: (layer, be[i], 0, 0)),
                      pl.BlockSpec((MOE_TILE, 1), lambda i, be, bv: (i, 0))],
            out_specs=pl.BlockSpec((MOE_TILE, D), lambda i, be, bv: (i, 0)),
            scratch_shapes=[pltpu.VMEM((D, De), BF16), pltpu.VMEM((D, De), BF16), pltpu.VMEM((De, D), BF16)]),
        out_shape=jax.ShapeDtypeStruct((cap, D), BF16),
        compiler_params=_params("arbitrary"),
        name="moe_experts",
    )(block_expert, block_valid, xs, w1, w3, w2, slot_gate)


def _router_kernel(x_ref, w_ref, b_ref, o_ref):
    logits = _dot(x_ref[...], w_ref[...]) + b_ref[...]
    lane = lax.broadcasted_iota(jnp.int32, logits.shape, 1)
    ninf = -jnp.inf

    def first_max(v):
        m = jnp.max(v, axis=-1, keepdims=True)
        return m, jnp.min(jnp.where(v == m, lane, logits.shape[1]), axis=-1, keepdims=True)

    is_g = lane < N_GROUPS
    m_g, grp = first_max(jnp.where(is_g, logits, ninf))
    p_grp = 1.0 / jnp.sum(jnp.where(is_g, jnp.exp(logits - m_g), 0.0), axis=-1, keepdims=True)
    lo = N_GROUPS + EXPERTS_PER_GROUP * grp
    le = jnp.where((lane >= lo) & (lane < lo + EXPERTS_PER_GROUP), logits, ninf)
    m1, i1 = first_max(le)
    m2, i2 = first_max(jnp.where(lane == i1, ninf, le))
    e = jnp.exp(m2 - m1)
    g1 = p_grp * (1.0 / (1.0 + e))
    g2 = p_grp * (e / (1.0 + e))
    out = jnp.where(lane == 0, (i1 - N_GROUPS).astype(F32),
                    jnp.where(lane == 1, (i2 - N_GROUPS).astype(F32),
                              jnp.where(lane == 2, g1, jnp.where(lane == 3, g2, 0.0))))
    o_ref[...] = out


def router(tok, w_router, b_router):
    N, D = tok.shape
    tm = 1024 if N % 1024 == 0 else ROW_TILE
    W = w_router.shape[1]
    return pl.pallas_call(
        _router_kernel,
        grid=(N // tm,),
        in_specs=[pl.BlockSpec((tm, D), lambda i: (i, 0)), pl.BlockSpec((D, W), lambda i: (0, 0)),
                  pl.BlockSpec((1, W), lambda i: (0, 0))],
        out_specs=pl.BlockSpec((tm, W), lambda i: (i, 0)),
        out_shape=jax.ShapeDtypeStruct((N, W), F32),
        compiler_params=_params("arbitrary"),
        name="router",
    )(tok, w_router, b_router)


def hier_moe(tok, w_router, b_router, layer, w1, w3, w2):
    N, D = tok.shape
    r = router(tok, w_router, b_router)
    eid = r[:, :TOP_K_IN_GROUP].astype(jnp.int32)
    gate = r[:, TOP_K_IN_GROUP:2 * TOP_K_IN_GROUP]
    A = N * TOP_K_IN_GROUP
    e_flat = eid.reshape(A)
    onehot = (e_flat[:, None] == jnp.arange(N_EXPERTS, dtype=jnp.int32)[None, :]).astype(jnp.int32)
    csum = jnp.cumsum(onehot, axis=0)
    rank = jnp.take_along_axis(csum, e_flat[:, None], axis=1)[:, 0] - 1
    counts = csum[-1]
    padded = (counts + MOE_TILE - 1) // MOE_TILE * MOE_TILE
    pend = jnp.cumsum(padded)
    pstart = pend - padded
    slot = pstart[e_flat] + rank
    n_blocks = -(-A // MOE_TILE) + N_EXPERTS
    cap = n_blocks * MOE_TILE
    slot_asg = jnp.full((cap,), -1, jnp.int32).at[slot].set(jnp.arange(A, dtype=jnp.int32))
    used = slot_asg >= 0
    slot_tok = jnp.where(used, slot_asg // TOP_K_IN_GROUP, jnp.arange(cap, dtype=jnp.int32) % N)
    slot_gate = jnp.where(used, gate.reshape(A)[jnp.maximum(slot_asg, 0)], 0.0)
    bstart = jnp.arange(n_blocks, dtype=jnp.int32) * MOE_TILE
    block_expert = jnp.minimum(jnp.sum((bstart[:, None] >= pend[None, :]).astype(jnp.int32), axis=1), N_EXPERTS - 1)
    block_valid = (bstart < pend[-1]).astype(jnp.int32)
    xs = tok[slot_tok]
    ys = moe_experts(xs, layer, w1, w3, w2, slot_gate[:, None], block_expert, block_valid)
    return ys[slot.reshape(N, TOP_K_IN_GROUP).T]


CONV_LANES = 128
HY_N2 = 128


def _conv3_time(x, w, S):
    T = x.shape[0]
    t = lax.broadcasted_iota(jnp.int32, x.shape, 0)
    prev = jnp.where((t == 0) | (t == S), 0.0, pltpu.roll(x, 1, 0))
    nxt = jnp.where((t == S - 1) | (t == T - 1), 0.0, pltpu.roll(x, T - 1, 0))
    return w[0:1] * prev + w[1:2] * x + w[2:3] * nxt


def _hy_pre_kernel(p_ref, w_ref, o_ref, *, S):
    o_ref[0, 0] = _conv3_time(p_ref[0], w_ref[...], S)


def hy_pre(p_hy, hy_short, S):
    B, T, C3 = p_hy.shape
    C = C3 // (HY_ORDER + 1)
    nj = C // CONV_LANES
    return pl.pallas_call(
        functools.partial(_hy_pre_kernel, S=S),
        grid=(B, HY_ORDER + 1, nj),
        in_specs=[pl.BlockSpec((1, T, CONV_LANES), lambda b, k, j: (b, 0, k * nj + j)),
                  pl.BlockSpec((CONV_W, CONV_LANES), lambda b, k, j: (0, k * nj + j))],
        out_specs=pl.BlockSpec((1, 1, T, CONV_LANES), lambda b, k, j: (k, b, 0, j)),
        out_shape=jax.ShapeDtypeStruct((HY_ORDER + 1, B, T, C), F32),
        compiler_params=_params("arbitrary", "arbitrary", "arbitrary"),
        name="hy_pre",
    )(p_hy, hy_short)


def _sc_mix_kernel(b_ref, c_ref, x_ref, w_ref, o_ref, *, S):
    o_ref[0] = (b_ref[0] * _conv3_time(c_ref[0] * x_ref[0], w_ref[...], S)).astype(o_ref.dtype)


def sc_mix(p_sc, sc_conv, S):
    B, T, C3 = p_sc.shape
    C = C3 // 3
    nj = C // CONV_LANES
    spec = lambda k: pl.BlockSpec((1, T, CONV_LANES), lambda b, j: (b, 0, k * nj + j))
    return pl.pallas_call(
        functools.partial(_sc_mix_kernel, S=S),
        grid=(B, nj),
        in_specs=[spec(0), spec(1), spec(2), pl.BlockSpec((CONV_W, CONV_LANES), lambda b, j: (0, j))],
        out_specs=pl.BlockSpec((1, T, CONV_LANES), lambda b, j: (b, 0, j)),
        out_shape=jax.ShapeDtypeStruct((B, T, C), BF16),
        compiler_params=_params("arbitrary", "arbitrary"),
        name="sc_mix",
    )(p_sc, p_sc, p_sc, sc_conv)


HY_EMB_PAD = 128


def _hy_filter_kernel(emb_ref, w1_ref, b1_ref, w2_ref, b2_ref, w3_ref, b3_ref, w4_ref, dl_ref, h_ref, s_ref):
    emb = emb_ref[...]
    a = jnp.sin(HY_SIN_FREQ * (_dot(emb, w1_ref[...]) + b1_ref[...]))
    a = jnp.sin(HY_SIN_FREQ * (_dot(a, w2_ref[...]) + b2_ref[...]))
    a = jnp.sin(HY_SIN_FREQ * (_dot(a, w3_ref[...]) + b3_ref[...]))
    t = emb[:, 0:1]
    h = _dot(a, w4_ref[...]) * jnp.exp(-t * dl_ref[...])
    h_ref[...] = h

    @pl.when(pl.program_id(1) == 0)
    def _():
        s_ref[...] = jnp.zeros_like(s_ref)

    s_ref[...] += jnp.sum(jnp.abs(h), axis=0, keepdims=True)


def hyena_filters(L, C, w1, b1, w2, b2, w3, b3, w4):
    t = jnp.linspace(0.0, 1.0, L, dtype=F32)[:, None]
    bands = (HY_EMB - 1) // 2
    freqs = jnp.linspace(1e-4, bands - 1, bands, dtype=F32)[None, :]
    w = (2.0 * math.pi / L) * jnp.arange(L, dtype=F32)[:, None]
    emb = jnp.concatenate([t, jnp.cos(freqs * w), -jnp.sin(freqs * w)], axis=-1)
    emb = jnp.pad(emb, ((0, 0), (0, HY_EMB_PAD - HY_EMB)))
    w1p = jnp.pad(w1, ((0, HY_EMB_PAD - HY_EMB), (0, 0)))
    F = w1.shape[1]
    CC = w4.shape[1]
    deltas = jnp.abs(jnp.linspace(math.log(HY_DECAY_TARGET) / HY_SLOW_DECAY,
                                  math.log(HY_DECAY_TARGET) / HY_FAST_DECAY, C, dtype=F32))
    dl = jnp.tile(deltas, CC // C).reshape(1, CC)
    tl = min(L, 1024)
    tn = 1024
    full = lambda shape: pl.BlockSpec(shape, lambda j, i: (0, 0))
    h, s = pl.pallas_call(
        _hy_filter_kernel,
        grid=(CC // tn, L // tl),
        in_specs=[pl.BlockSpec((tl, HY_EMB_PAD), lambda j, i: (i, 0)),
                  full((HY_EMB_PAD, F)), full((1, F)), full((F, F)), full((1, F)), full((F, F)), full((1, F)),
                  pl.BlockSpec((F, tn), lambda j, i: (0, j)), pl.BlockSpec((1, tn), lambda j, i: (0, j))],
        out_specs=[pl.BlockSpec((tl, tn), lambda j, i: (i, j)), pl.BlockSpec((1, tn), lambda j, i: (0, j))],
        out_shape=[jax.ShapeDtypeStruct((L, CC), F32), jax.ShapeDtypeStruct((1, CC), F32)],
        compiler_params=_params("arbitrary", "arbitrary"),
        name="hy_filter",
    )(emb, w1p, b1.reshape(1, F), w2, b2.reshape(1, F), w3, b3.reshape(1, F), w4, dl)
    tot = s.reshape(HY_ORDER, 2, C).sum(axis=1, keepdims=True) + NORM_EPS
    inv = (1.0 / jnp.broadcast_to(tot, (HY_ORDER, 2, C))).reshape(1, CC)
    return h, inv


def dft_tables(L):
    n = 2 * L
    N1 = n // HY_N2
    A = N1 // 2
    NC = min(N1, -(-(A + 1) // 8) * 8)
    c = jnp.arange(NC, dtype=jnp.int32)
    a = jnp.arange(A, dtype=jnp.int32)
    ang1 = ((c[:, None] * a[None, :]) % N1).astype(F32) * (2.0 * math.pi / N1)
    fs = jnp.concatenate([jnp.cos(ang1), -jnp.sin(ang1)], axis=0)
    wr = jnp.where((c == 0) | (c == A), 1.0, jnp.where(c < A, 2.0, 0.0))[:, None]
    fi = jnp.concatenate([wr * jnp.cos(ang1), -wr * jnp.sin(ang1)], axis=0).T
    b = jnp.arange(HY_N2, dtype=jnp.int32)
    d = jnp.arange(HY_N2, dtype=jnp.int32)
    k2 = (b[None, None, :] * (c[:, None, None] + N1 * d[None, :, None])) % n
    ang2 = k2.astype(F32) * (2.0 * math.pi / n)
    cs, sn = jnp.cos(ang2), jnp.sin(ang2)
    gf = jnp.concatenate([jnp.concatenate([cs, sn], axis=2),
                          jnp.concatenate([-sn, cs], axis=2)], axis=1)
    return fs.astype(BF16), fi.astype(BF16), gf.astype(BF16), gf.transpose(0, 2, 1).astype(BF16)


DFT_SUB = 8


def _time_view(x):
    K, B, T, C = x.shape
    return x.reshape(K, B, T // HY_N2, HY_N2 // DFT_SUB, DFT_SUB, C)


LANES = 128


def _strided_rows(refs, first, count, stride):
    return jnp.concatenate([r[pl.ds(first, count, stride=stride), :] for r in refs], axis=1)


def _lane_tile_specs(block, index, n_tiles):
    def spec(t):
        def idx(*g):
            *lead, j = index(*g)
            return (*lead, j * n_tiles + t)
        return pl.BlockSpec(block + (LANES,), idx)
    return [spec(t) for t in range(n_tiles)]


def _dft_s1_kernel(f_ref, *refs):
    *x_refs, o_ref = refs
    A = x_refs[0].shape[0]
    flat = [r.reshape(A * DFT_SUB, LANES) for r in x_refs]
    NC = o_ref.shape[3]
    for bl in range(DFT_SUB):
        y = _dot(f_ref[...], _strided_rows(flat, bl, A, DFT_SUB))
        o_ref[0, 0, bl] = y[:NC]
        o_ref[0, 1, bl] = y[NC:]


def dft_s1(fs, src, k, tc):
    _, B, _, C = src.shape
    M, A = fs.shape
    nt = tc // LANES
    x_specs = _lane_tile_specs((None, None, A, None, DFT_SUB), lambda b, j, g: (k, b, 0, g, 0, j), nt)
    return pl.pallas_call(
        _dft_s1_kernel,
        grid=(B, C // tc, HY_N2 // DFT_SUB),
        in_specs=[pl.BlockSpec((M, A), lambda b, j, g: (0, 0))] + x_specs,
        out_specs=pl.BlockSpec((1, 2, DFT_SUB, M // 2, tc), lambda b, j, g: (b, 0, g, 0, j)),
        out_shape=jax.ShapeDtypeStruct((B, 2, HY_N2, M // 2, C), F32),
        compiler_params=_params("arbitrary", "arbitrary", "arbitrary"),
        name="dft_s1",
    )(fs, *([_time_view(src)] * nt))


def _pair_rows(refs):
    return [r.reshape(2 * r.shape[1] * DFT_SUB, LANES) for r in refs]


def _gather_pair(flat, n_rows, sub):
    return jnp.concatenate([_strided_rows(flat, sub, n_rows, DFT_SUB),
                            _strided_rows(flat, n_rows * DFT_SUB + sub, n_rows, DFT_SUB)], axis=0)


def _dft_mid_kernel(gf_ref, gi_ref, *refs):
    *y_refs, h_ref, o_ref = refs
    flat = _pair_rows(y_refs)
    group = 4
    for c0 in range(0, DFT_SUB, group):
        cls = range(c0, c0 + group)
        xs = [_dot(gf_ref[cl], _gather_pair(flat, HY_N2, cl)) for cl in cls]
        ps = []
        for cl, x in zip(cls, xs):
            xr, xi = x[:HY_N2], x[HY_N2:]
            hr, hi = h_ref[0, 0, 0, cl], h_ref[0, 1, 0, cl]
            ps.append(jnp.concatenate([xr * hr - xi * hi, xr * hi + xi * hr], axis=0).astype(BF16))
        qs = [_dot(gi_ref[cl], p) for cl, p in zip(cls, ps)]
        for cl, q in zip(cls, qs):
            o_ref[0, 0, cl] = q[:HY_N2]
            o_ref[0, 1, cl] = q[HY_N2:]


def dft_mid(gf, gi, y, hs, order, tc):
    B, _, N2, NC, C = y.shape
    nt = tc // LANES
    y6 = y.reshape(B, 2, N2, NC // DFT_SUB, DFT_SUB, C)
    hs6 = hs.reshape(HY_ORDER, 2, NC // DFT_SUB, DFT_SUB, N2, C)
    y_specs = _lane_tile_specs((None, 2, N2, None, DFT_SUB), lambda c, j, b: (b, 0, 0, c, 0, j), nt)
    return pl.pallas_call(
        _dft_mid_kernel,
        grid=(NC // DFT_SUB, C // tc, B),
        in_specs=[pl.BlockSpec((DFT_SUB, 2 * N2, 2 * N2), lambda c, j, b: (c, 0, 0)),
                  pl.BlockSpec((DFT_SUB, 2 * N2, 2 * N2), lambda c, j, b: (c, 0, 0))] + y_specs +
                 [pl.BlockSpec((1, 2, 1, DFT_SUB, N2, tc), lambda c, j, b: (order, 0, c, 0, 0, j))],
        out_specs=pl.BlockSpec((1, 2, DFT_SUB, N2, tc), lambda c, j, b: (b, 0, c, 0, j)),
        out_shape=jax.ShapeDtypeStruct((B, 2, NC, N2, C), F32),
        compiler_params=_params("arbitrary", "arbitrary", "arbitrary"),
        name="dft_mid",
    )(gf, gi, *([y6] * nt), hs6)


def _dft_filt_kernel(gf_ref, *refs):
    *y_refs, inv_ref, o_ref = refs
    nt = len(y_refs) // 2
    flat_f, flat_b = _pair_rows(y_refs[:nt]), _pair_rows(y_refs[nt:])
    inv = inv_ref[...]
    for cl in range(DFT_SUB):
        xf = _dot(gf_ref[cl], _gather_pair(flat_f, HY_N2, cl))
        xb = _dot(gf_ref[cl], _gather_pair(flat_b, HY_N2, cl))
        o_ref[0, 0, 0, cl] = (xf[:HY_N2] + xb[:HY_N2]) * inv
        o_ref[0, 1, 0, cl] = (xf[HY_N2:] - xb[HY_N2:]) * inv


def dft_filter_spectrum(gf, y, inv, C, tc):
    _, _, N2, NC, CC = y.shape
    nj = C // tc
    nt = tc // LANES
    y6 = y.reshape(1, 2, N2, NC // DFT_SUB, DFT_SUB, CC)
    block = (None, 2, N2, None, DFT_SUB)
    yf_specs = _lane_tile_specs(block, lambda c, o, j: (0, 0, 0, c, 0, (2 * o) * nj + j), nt)
    yb_specs = _lane_tile_specs(block, lambda c, o, j: (0, 0, 0, c, 0, (2 * o + 1) * nj + j), nt)
    out = pl.pallas_call(
        _dft_filt_kernel,
        grid=(NC // DFT_SUB, HY_ORDER, nj),
        in_specs=[pl.BlockSpec((DFT_SUB, 2 * N2, 2 * N2), lambda c, o, j: (c, 0, 0))] + yf_specs + yb_specs +
                 [pl.BlockSpec((1, tc), lambda c, o, j: (0, (2 * o) * nj + j))],
        out_specs=pl.BlockSpec((1, 2, 1, DFT_SUB, N2, tc), lambda c, o, j: (o, 0, c, 0, 0, j)),
        out_shape=jax.ShapeDtypeStruct((HY_ORDER, 2, NC // DFT_SUB, DFT_SUB, N2, C), F32),
        compiler_params=_params("arbitrary", "arbitrary", "arbitrary"),
        name="dft_filter_spectrum",
    )(gf, *([y6] * (2 * nt)), inv)
    return out.reshape(HY_ORDER, 2, NC, N2, C)


def _dft_i2_kernel(f_ref, *refs, inv_n):
    nt = (len(refs) - 2) // 3
    q_refs, z_refs, g_refs = refs[:nt], refs[nt:2 * nt], refs[2 * nt:3 * nt]
    bias_ref, o_ref = refs[3 * nt:]
    A = z_refs[0].shape[0]
    NC = q_refs[0].shape[1]
    q_flat = _pair_rows(q_refs)
    z_flat = [r.reshape(A * DFT_SUB, LANES) for r in z_refs]
    g_flat = [r.reshape(A * DFT_SUB, LANES) for r in g_refs]
    for bl in range(DFT_SUB):
        y = _dot(f_ref[...], _gather_pair(q_flat, NC, bl))
        z = _strided_rows(z_flat, bl, A, DFT_SUB)
        g = _strided_rows(g_flat, bl, A, DFT_SUB)
        o_ref[:, 0, bl, :] = g * (y * inv_n + bias_ref[...] * z)


def dft_i2(fi, q, zsrc, kz, gsrc, kg, bias, tc):
    A, M = fi.shape
    B, _, NC, _, C = q.shape
    ng = HY_N2 // DFT_SUB
    nt = tc // LANES
    q6 = q.reshape(B, 2, NC, ng, DFT_SUB, C)
    q_specs = _lane_tile_specs((None, 2, NC, None, DFT_SUB), lambda b, j, g: (b, 0, 0, g, 0, j), nt)
    tspecs = lambda k: _lane_tile_specs((None, None, A, None, DFT_SUB), lambda b, j, g: (k, b, 0, g, 0, j), nt)
    out = pl.pallas_call(
        functools.partial(_dft_i2_kernel, inv_n=1.0 / (2 * A * HY_N2)),
        grid=(B, C // tc, ng),
        in_specs=[pl.BlockSpec((A, M), lambda b, j, g: (0, 0))] + q_specs + tspecs(kz) + tspecs(kg) +
                 [pl.BlockSpec((1, tc), lambda b, j, g: (0, j))],
        out_specs=pl.BlockSpec((None, A, 1, DFT_SUB, tc), lambda b, j, g: (b, 0, g, 0, j)),
        out_shape=jax.ShapeDtypeStruct((B, A, ng, DFT_SUB, C), F32),
        compiler_params=_params("arbitrary", "arbitrary", "arbitrary"),
        name="dft_i2",
    )(fi, *([q6] * nt), *([_time_view(zsrc)] * nt), *([_time_view(gsrc)] * nt), bias)
    return out.reshape(B, A * HY_N2, C)


def _hy_ctx_kernel(fd_ref, fdi_ref, pc_ref, h_ref, bias_ref, o_ref, *, n):
    hp = lax.Precision.HIGHEST
    dot = lambda a, b: jnp.dot(a, b, preferred_element_type=F32, precision=hp)
    fd = fd_ref[...]
    z = pc_ref[0, 0]
    for o in range(HY_ORDER):
        hf = dot(fd, h_ref[2 * o])
        hb = dot(fd, h_ref[2 * o + 1])
        hr = hf[:n] + hb[:n]
        hi = hf[n:] - hb[n:]
        x = dot(fd, z)
        xr, xi = x[:n], x[n:]
        p = jnp.concatenate([xr * hr - xi * hi, xr * hi + xi * hr], axis=0)
        y = dot(fdi_ref[...], p) * (1.0 / n)
        z = pc_ref[o + 1, 0] * (y + bias_ref[o:o + 1] * z)
    o_ref[0] = z


def hy_ctx(pc, filt_ctx, hy_bias, S):
    _, B, T, C = pc.shape
    Lc = T - S
    n = 2 * Lc
    f = jnp.arange(n, dtype=jnp.int32)
    t = jnp.arange(Lc, dtype=jnp.int32)
    ang = ((f[:, None] * t[None, :]) % n).astype(F32) * (2.0 * math.pi / n)
    fd = jnp.concatenate([jnp.cos(ang), -jnp.sin(ang)], axis=0)
    h4 = filt_ctx.reshape(Lc, 2 * HY_ORDER, C).transpose(1, 0, 2)
    tc = 256
    return pl.pallas_call(
        functools.partial(_hy_ctx_kernel, n=n),
        grid=(B, C // tc),
        in_specs=[pl.BlockSpec((2 * n, Lc), lambda b, j: (0, 0)),
                  pl.BlockSpec((Lc, 2 * n), lambda b, j: (0, 0)),
                  pl.BlockSpec((HY_ORDER + 1, 1, Lc, tc), lambda b, j: (0, b, S // Lc, j)),
                  pl.BlockSpec((2 * HY_ORDER, Lc, tc), lambda b, j: (0, 0, j)),
                  pl.BlockSpec((HY_ORDER, tc), lambda b, j: (0, j))],
        out_specs=pl.BlockSpec((1, Lc, tc), lambda b, j: (b, 0, j)),
        out_shape=jax.ShapeDtypeStruct((B, Lc, C), F32),
        compiler_params=_params("arbitrary", "arbitrary"),
        name="hy_ctx",
    )(fd, fd.T, pc, h4, hy_bias)


def hyena_latent(pc, filt, hy_bias, S, tabs):
    fs, fi, gf, gi = tabs
    _, B, T, C = pc.shape
    h_raw, inv = filt
    hy = dft_s1(fs, h_raw.reshape(1, 1, S, 2 * HY_ORDER * C), 0, 512)
    hs = dft_filter_spectrum(gf, hy, inv, C, 512)
    zsrc, kz = pc, 0
    for o in range(HY_ORDER):
        y = dft_s1(fs, zsrc, kz, 512)
        q = dft_mid(gf, gi, y, hs, o, 512)
        z = dft_i2(fi, q, zsrc, kz, pc, o + 1, hy_bias[o].reshape(1, C), 512)
        zsrc, kz = z.reshape(1, B, S, C), 0
    return z


def _flat(a):
    return a.reshape(a.shape[0] * a.shape[1], a.shape[2])


def _proj(h, w, out_dtype, tn):
    B, T, D = h.shape
    M = B * T
    tm = 1024 if M % 1024 == 0 else ROW_TILE
    return mm(_flat(h), w, out_dtype, tm, tn).reshape(B, T, w.shape[1])


def even_layer(h, xs, modtab, gain_ffn, S, w_in, w_out, rpb, gk_up, gk_bias, gnorm):
    B, T, D = xs.shape
    na_w = w_in.shape[1] - 2 * GLA_HEADS * GLA_DK - 2 * (D // 2) - 2 * GLA_LOWRANK
    na_w //= 3
    qk_w = GLA_HEADS * GLA_DK
    v_w = D // 2
    dv = v_w // GLA_HEADS
    o = 3 * na_w
    w_na = w_in[:, :o].astype(BF16)
    wq, wk = w_in[:, o:o + qk_w], w_in[:, o + qk_w:o + 2 * qk_w]
    wv = w_in[:, o + 2 * qk_w:o + 2 * qk_w + v_w]
    wg = w_in[:, o + 2 * qk_w + v_w:o + 2 * qk_w + 2 * v_w]
    wlr = w_in[:, o + 2 * qk_w + 2 * v_w:]
    per_head = lambda w, width: w.reshape(D, GLA_HEADS, width)
    w_gl = jnp.concatenate([per_head(wq, GLA_DK), per_head(_rot_partner_cols(wq), GLA_DK),
                            per_head(wk, GLA_DK), per_head(_rot_partner_cols(wk), GLA_DK),
                            per_head(wv, dv)], axis=-1).reshape(D, GLA_HEADS * GLA_HEAD_COLS).astype(BF16)
    w_g = wg.astype(BF16)
    w_lr = jnp.pad(wlr, ((0, 0), (0, 128 - 2 * GLA_LOWRANK))).astype(BF16)

    qkv = _proj(h, w_na, BF16, 1024)
    gl = _proj(h, w_gl, F32, 768)
    g = _proj(h, w_g, F32, 1024)
    lr = _proj(h, w_lr, F32, 128)

    ya = na_attention(qkv, na_bias_table(rpb, S), S)
    cos_tab, sin_tab = rope_tables(S, T)
    up_tied = _tie_pairs(gk_up.astype(F32)).transpose(2, 0, 1, 3)
    up_tied = jnp.stack([jnp.pad(up_tied[:, 0], ((0, 0), (0, 128 - GLA_LOWRANK), (0, 0))),
                         jnp.pad(up_tied[:, 1], ((0, 0), (GLA_LOWRANK, 128 - 2 * GLA_LOWRANK), (0, 0)))],
                        axis=1)
    bias_tied = _tie_pairs(gk_bias.astype(F32)).transpose(1, 0, 2)[:, :, None, :]
    o_f, o_b = gla_scan(gl, lr, cos_tab, sin_tab, up_tied, bias_tied, S)
    w_out_b = w_out.astype(BF16)
    return mm_res([ya], [w_out_b[:na_w], w_out_b[na_w:]], xs, modtab, gain_ffn, S, gla=(o_f, o_b, g, gnorm))


def odd_layer(h, xs, modtab, gain_ffn, S, ctx_live, w_in, w_out, hy_short, hyp, hy_bias, sc_conv, tabs):
    B, T, D = xs.shape
    C = D // 2
    n_hy = (HY_ORDER + 1) * C
    w = w_in.astype(BF16)
    pc = hy_pre(_proj(h, w[:, :n_hy], F32, 1024), hy_short, S)
    y_sc = sc_mix(_proj(h, w[:, n_hy:], F32, 1024), sc_conv, S)
    z_lat = hyena_latent(pc, hyena_filters(S, C, *hyp), hy_bias, S, tabs)
    if ctx_live:
        h_ctx, inv_ctx = hyena_filters(T - S, C, *hyp)
        z_ctx = hy_ctx(pc, h_ctx * inv_ctx, hy_bias, S)
    else:
        z_ctx = jnp.zeros((B, T - S, C), F32)
    z = jnp.concatenate([z_lat, z_ctx], axis=1)
    w_out_b = w_out.astype(BF16)
    return mm_res([z, y_sc], [w_out_b[:C], w_out_b[C:]], xs, modtab, gain_ffn, S)


def kernel(x, c, ctx, c_ctx, w_mod, b_mod, norm_mix, norm_ffn, norm_final, w_in_even, w_out_even, na_rpb,
           gla_gk_up, gla_gk_bias, gla_norm, w_in_odd, w_out_odd, hy_short, hy_w1, hy_b1, hy_w2, hy_b2,
           hy_w3, hy_b3, hy_w4, hy_bias, sc_conv, moe_w_group, moe_b_group, moe_w_expert, moe_b_expert,
           moe_w1, moe_w3, moe_w2):
    B, S, D = x.shape
    Lc = ctx.shape[1]
    T = S + Lc
    depth = w_mod.shape[0]
    last_even = 2 * ((depth - 1) // 2)
    tabs = dft_tables(S)

    cvec = jnp.concatenate([c, c_ctx[None, :], jnp.zeros((8 - B - 1, D), F32)], axis=0)
    cvec = cvec * jax.nn.sigmoid(cvec)
    modtabs = []
    for l in range(depth):
        mod = mm(cvec, w_mod, F32, 8, 1024, layer=l)[:B + 1] + b_mod[l]
        mod = mod.reshape(B + 1, 6, D)
        cx = jnp.broadcast_to(mod[B][None], (B, 6, D))
        modtabs.append(jnp.pad(jnp.stack([mod[:B], cx], axis=1), ((0, 0), (0, 0), (0, 2), (0, 0))))

    xs, h = embed(x, ctx, norm_mix[0], modtabs[0])
    for l in range(depth):
        modtab = modtabs[l]
        if l % 2 == 0:
            e = l // 2
            xs, tok = even_layer(h, xs, modtab, norm_ffn[l], S, w_in_even[e], w_out_even[e], na_rpb[e],
                                 gla_gk_up[e], gla_gk_bias[e], gla_norm[e])
        else:
            o = l // 2
            hyp = (hy_w1[o], hy_b1[o], hy_w2[o], hy_b2[o], hy_w3[o], hy_b3[o], hy_w4[o])
            xs, tok = odd_layer(h, xs, modtab, norm_ffn[l], S, l < last_even, w_in_odd[o], w_out_odd[o],
                                hy_short[o], hyp, hy_bias[o], sc_conv[o], tabs)

        w_router = jnp.pad(jnp.concatenate([moe_w_group[l], moe_w_expert[l]], axis=1),
                           ((0, 0), (0, 128 - N_GROUPS - N_EXPERTS))).astype(BF16)
        b_router = jnp.pad(jnp.concatenate([moe_b_group[l], moe_b_expert[l]]),
                           (0, 128 - N_GROUPS - N_EXPERTS)).reshape(1, 128)
        pair = hier_moe(_flat(tok), w_router, b_router, l, moe_w1, moe_w3, moe_w2)
        if l + 1 < depth:
            xs, h = moe_combine(xs, pair, modtab, norm_mix[l + 1], modtabs[l + 1], S, False)
        else:
            (out,) = moe_combine(xs, pair, modtab, norm_final, modtab, S, True)
    return out
```
